```python
import jax, jax.numpy as jnp
from jax import lax
import numpy as np

D_MODEL = 1024
BATCH = 4
SEQ = 8192
DEPTH = 2

POOL_WIDTH = D_MODEL // 2
POOL_WINDOWS = (2, 4, 8, 16)
N_POOL_GROUPS = len(POOL_WINDOWS)
POOL_GROUP = POOL_WIDTH // N_POOL_GROUPS
N_HEADS = 8
HEAD_DIM = 64
ATTN_WIDTH = N_HEADS * HEAD_DIM
Q_BLOCK = 128
IN_WIDTH = 2 * POOL_WIDTH + 4 * ATTN_WIDTH + 2 * D_MODEL
RMS_EPS = 1e-6

kernel_name = "hybrid_pool_stickbreak_gated"


def rms_norm(x, g):
    xf = x.astype(jnp.float32)
    y = xf * lax.rsqrt(jnp.mean(xf * xf, axis=-1, keepdims=True) + RMS_EPS)
    return (y * g.astype(jnp.float32)).astype(x.dtype)


def multiscale_pool(u, w_group, scale):
    B, S, _ = u.shape
    grp = u.astype(jnp.float32).reshape(B, S, N_POOL_GROUPS, POOL_GROUP)
    cs = jnp.cumsum(grp, axis=1)
    pos = jnp.arange(S)
    means = []
    for g, w in enumerate(POOL_WINDOWS):
        c = cs[:, :, g]
        prev = jnp.pad(c, ((0, 0), (w, 0), (0, 0)))[:, :S]
        cnt = jnp.minimum(pos + 1, w).astype(jnp.float32)[None, :, None]
        means.append((c - prev) / cnt)
    pooled = jnp.stack(means, axis=2) - grp
    mixed = jnp.einsum('bsgc,gcd->bsgd', pooled, w_group.astype(jnp.float32))
    return (mixed.reshape(B, S, POOL_WIDTH) * scale.astype(jnp.float32)).astype(u.dtype)


def stick_breaking_attention(q, k, v):
    B, S, H, Dh = q.shape
    n_blocks = S // Q_BLOCK
    qb = q.reshape(B, n_blocks, Q_BLOCK, H, Dh).transpose(1, 0, 2, 3, 4)
    kf = k.astype(jnp.float32)
    vf = v.astype(jnp.float32)
    key_pos = jnp.arange(S)
    inv_sqrt_d = 1.0 / float(np.sqrt(Dh))

    def one_block(args):
        q_blk, blk = args
        logits = jnp.einsum('bqhd,bkhd->bhqk', q_blk.astype(jnp.float32), kf) * inv_sqrt_d
        q_pos = blk * Q_BLOCK + jnp.arange(Q_BLOCK)
        mask = (key_pos[None, :] < q_pos[:, None])[None, None]
        log_beta = jax.nn.log_sigmoid(logits)
        log_1m_beta = jnp.where(mask, jax.nn.log_sigmoid(-logits), 0.0)
        later = lax.cumsum(log_1m_beta, axis=3, reverse=True) - log_1m_beta
        wts = jnp.where(mask, jnp.exp(log_beta + later), 0.0)
        return jnp.einsum('bhqk,bkhd->bqhd', wts, vf)

    out = lax.map(one_block, (qb, jnp.arange(n_blocks)))
    return out.transpose(1, 0, 2, 3, 4).reshape(B, S, H, Dh).astype(q.dtype)


def setup_inputs(seed: int = 0) -> dict:
    key = jax.random.key(seed)
    ks = jax.random.split(key, 11)
    f32 = jnp.float32
    x = jax.random.normal(ks[0], (BATCH, SEQ, D_MODEL), f32)
    norm_g = 1.0 + 0.05 * jax.random.normal(ks[1], (DEPTH, D_MODEL), f32)
    w_in = jax.random.normal(ks[2], (DEPTH, D_MODEL, IN_WIDTH), f32) * D_MODEL ** -0.5
    b_gate = 0.01 * jax.random.normal(ks[3], (DEPTH, 2 * D_MODEL), f32)
    pool_w = jax.random.normal(ks[4], (DEPTH, N_POOL_GROUPS, POOL_GROUP, POOL_GROUP), f32) * POOL_GROUP ** -0.5
    pool_scale = 1.0 + 0.1 * jax.random.normal(ks[5], (DEPTH, POOL_WIDTH), f32)
    w_pool_up = jax.random.normal(ks[6], (DEPTH, POOL_WIDTH, D_MODEL), f32) * POOL_WIDTH ** -0.5
    w_attn_up = jax.random.normal(ks[7], (DEPTH, ATTN_WIDTH, D_MODEL), f32) * ATTN_WIDTH ** -0.5
    w_out = jax.random.normal(ks[8], (DEPTH, D_MODEL, D_MODEL), f32) * D_MODEL ** -0.5
    final_g = 1.0 + 0.05 * jax.random.normal(ks[9], (D_MODEL,), f32)
    return {"x": x, "norm_g": norm_g, "w_in": w_in, "b_gate": b_gate, "pool_w": pool_w,
            "pool_scale": pool_scale, "w_pool_up": w_pool_up, "w_attn_up": w_attn_up,
            "w_out": w_out, "final_g": final_g}


def reference(x, norm_g, w_in, b_gate, pool_w, pool_scale, w_pool_up, w_attn_up, w_out, final_g):
    B, S, D = x.shape
    splits = np.cumsum([POOL_WIDTH, POOL_WIDTH, ATTN_WIDTH, ATTN_WIDTH, ATTN_WIDTH, ATTN_WIDTH]).tolist()
    for l in range(DEPTH):
        h = rms_norm(x, norm_g[l])
        proj = jnp.einsum('bsd,de->bse', h, w_in[l])
        u_pool, z_pool, q, k, v, z_attn, gate_logits = jnp.split(proj, splits, axis=-1)
        y_pool = multiscale_pool(u_pool, pool_w[l], pool_scale[l]) * jax.nn.silu(z_pool)
        attn = stick_breaking_attention(q.reshape(B, S, N_HEADS, HEAD_DIM),
                                        k.reshape(B, S, N_HEADS, HEAD_DIM),
                                        v.reshape(B, S, N_HEADS, HEAD_DIM))
        y_attn = attn.reshape(B, S, ATTN_WIDTH) * jax.nn.silu(z_attn)
        gates = jax.nn.sigmoid(gate_logits + b_gate[l]).reshape(B, S, 2, D)
        merged = (gates[:, :, 0] * jnp.einsum('bsp,pd->bsd', y_pool, w_pool_up[l])
                  + gates[:, :, 1] * jnp.einsum('bsa,ad->bsd', y_attn, w_attn_up[l]))
        x = x + jnp.einsum('bsd,de->bse', merged, w_out[l])
    return rms_norm(x, final_g)
```

```python
import functools

import jax
import jax.numpy as jnp
from jax import lax
from jax.experimental import pallas as pl
from jax.experimental.pallas import tpu as pltpu

D_MODEL = 1024
POOL_WIDTH = 512
POOL_WINDOWS = (2, 4, 8, 16)
POOL_GROUP = 128
N_HEADS = 8
HEAD_DIM = 64
ATTN_WIDTH = N_HEADS * HEAD_DIM
RMS_EPS = 1e-6

LANES = 128
HALO = 16
PROJ_ROWS = 512
MERGE_ROWS = 256
ATTN_ROWS = 512
SUB = 64
WIN = 256
HALF = WIN // 2
LOG_WEIGHT_FLOOR = -110.0
VMEM_LIMIT = 48 * 1024 * 1024

F32 = jnp.float32
BF16 = jnp.bfloat16


def _rms_norm(x, g):
    ms = jnp.mean(x * x, axis=-1, keepdims=True)
    return x * lax.rsqrt(ms + RMS_EPS) * g


def _silu(z):
    return z * jax.nn.sigmoid(z)


def _dot(a, b):
    return jnp.dot(a, b, preferred_element_type=F32)


def _proj_pool_kernel(x_ref, g_ref, w_ref, pw_ref, ps_ref,
                      yp_ref, q_ref, k_ref, v_ref, ext_ref):
    s = pl.program_id(1)
    rows = x_ref.shape[0]
    h = _rms_norm(x_ref[...], g_ref[...]).astype(BF16)

    pw, aw = POOL_WIDTH, ATTN_WIDTH
    u = _dot(h, w_ref[:, 0:pw])
    z = _dot(h, w_ref[:, pw:2 * pw])
    q_ref[...] = _dot(h, w_ref[:, 2 * pw:2 * pw + aw]).astype(BF16)
    k_ref[...] = _dot(h, w_ref[:, 2 * pw + aw:2 * pw + 2 * aw]).astype(BF16)
    v_ref[...] = _dot(h, w_ref[:, 2 * pw + 2 * aw:2 * pw + 3 * aw]).astype(BF16)

    @pl.when(s == 0)
    def _():
        ext_ref[0:HALO, :] = jnp.zeros((HALO, pw), F32)

    ext_ref[HALO:HALO + rows, :] = u
    pos = s * rows + lax.broadcasted_iota(jnp.int32, (rows, 1), 0)
    for g, w in enumerate(POOL_WINDOWS):
        cols = slice(g * POOL_GROUP, (g + 1) * POOL_GROUP)
        ug = u[:, cols]
        acc = ug
        for i in range(1, w):
            acc = acc + ext_ref[HALO - i:HALO - i + rows, cols]
        cnt = jnp.minimum(pos + 1, w).astype(F32)
        pooled = acc / cnt - ug
        mixed = _dot(pooled.astype(BF16), pw_ref[g])
        y = mixed * ps_ref[:, cols] * _silu(z[:, cols])
        yp_ref[:, cols] = y.astype(BF16)
    ext_ref[0:HALO, :] = ext_ref[rows:rows + HALO, :]


def _proj_pool(x, g, w_a, pool_w, pool_scale):
    B, S, D = x.shape
    rows = PROJ_ROWS
    n_cols = w_a.shape[1]
    act = jax.ShapeDtypeStruct((B, S, ATTN_WIDTH), BF16)
    row_spec = lambda width: pl.BlockSpec((None, rows, width), lambda b, s: (b, s, 0))
    full = lambda shape: pl.BlockSpec(shape, lambda b, s: (0,) * len(shape))
    return pl.pallas_call(
        _proj_pool_kernel,
        grid=(B, S // rows),
        in_specs=[row_spec(D), full((1, D)), full((D, n_cols)),
                  full((len(POOL_WINDOWS), POOL_GROUP, POOL_GROUP)), full((1, POOL_WIDTH))],
        out_specs=[row_spec(POOL_WIDTH), row_spec(ATTN_WIDTH), row_spec(ATTN_WIDTH), row_spec(ATTN_WIDTH)],
        out_shape=[jax.ShapeDtypeStruct((B, S, POOL_WIDTH), BF16), act, act, act],
        scratch_shapes=[pltpu.VMEM((rows + HALO, POOL_WIDTH), F32)],
        compiler_params=pltpu.CompilerParams(
            dimension_semantics=("arbitrary", "arbitrary"), vmem_limit_bytes=VMEM_LIMIT),
        name="proj_pool",
    )(x, g, w_a, pool_w, pool_scale)


def _attn_kernel(q_ref, k_ref, v_ref, sufw_ref, o_ref, acc_ref, carry_ref):
    step = pl.program_id(2)
    lane = lax.broadcasted_iota(jnp.int32, (SUB, LANES), 1)
    row_in_sub = lax.broadcasted_iota(jnp.int32, (2 * SUB, WIN), 0) & (SUB - 1)
    col = lax.broadcasted_iota(jnp.int32, (2 * SUB, WIN), 1)

    def sub_block(i, _):
        row0 = pl.multiple_of(i * SUB, SUB)
        q0 = step * ATTN_ROWS + row0
        q2 = q_ref[pl.ds(row0, SUB), :]
        zero = jnp.zeros_like(q2)
        qs = jnp.concatenate([jnp.where(lane < HEAD_DIM, q2, zero),
                              jnp.where(lane >= HEAD_DIM, q2, zero)], axis=0)
        t_pos = q0 + row_in_sub

        def window(start, hi, first):
            start = pl.multiple_of(start, SUB)
            k2 = k_ref[pl.ds(start, WIN), :]
            v2 = v_ref[pl.ds(start, WIN), :]
            logits = lax.dot_general(qs, k2, (((1,), (1,)), ((), ())),
                                     preferred_element_type=F32)
            valid = (col + start) < jnp.minimum(t_pos, hi)
            softplus = jnp.log(1.0 + jnp.exp(-jnp.abs(logits)))
            log_beta = jnp.minimum(logits, 0.0) - softplus
            l1m = jnp.where(valid, log_beta - logits, 0.0)
            l1m_hi = l1m.astype(BF16)
            l1m_lo = (l1m - l1m_hi.astype(F32)).astype(BF16)
            x = jnp.concatenate(
                [jnp.concatenate([l1m_hi[:, :HALF], l1m_lo[:, :HALF]], axis=1),
                 jnp.concatenate([l1m_hi[:, HALF:], l1m_lo[:, HALF:]], axis=1)], axis=0)
            r = _dot(x, sufw_ref[...])
            suf_a, tot_a = r[:2 * SUB, :HALF], r[:2 * SUB, HALF:]
            suf_b, tot_b = r[2 * SUB:, :HALF], r[2 * SUB:, HALF:]
            if first:
                later = jnp.concatenate([suf_a + tot_b, suf_b], axis=1)
                new_carry = tot_a + tot_b
            else:
                carry = carry_ref[...]
                later = jnp.concatenate([suf_a + (tot_b + carry), suf_b + carry], axis=1)
                new_carry = carry + (tot_a + tot_b)
            p = jnp.where(valid, jnp.exp(log_beta + later), 0.0).astype(BF16)
            pv = _dot(p, v2)
            acc_ref[...] = pv if first else acc_ref[...] + pv
            carry_ref[...] = new_carry
            return jnp.max(new_carry)

        start0 = jnp.maximum(q0 - (WIN - SUB), 0)
        m0 = window(start0, q0 + SUB, True)

        def cond(c):
            start, m = c
            return jnp.logical_and(start > 0, m > LOG_WEIGHT_FLOOR)

        def body(c):
            start, _ = c
            nxt = jnp.maximum(start - WIN, 0)
            return nxt, window(nxt, start, False)

        lax.while_loop(cond, body, (start0, m0))
        acc = acc_ref[...]
        o_ref[pl.ds(row0, SUB), :] = jnp.where(lane < HEAD_DIM, acc[:SUB], acc[SUB:]).astype(BF16)
        return 0

    lax.fori_loop(0, ATTN_ROWS // SUB, sub_block, 0)


def _suffix_sum_weights():
    j = jnp.arange(HALF)[:, None]
    c = jnp.arange(HALF)[None, :]
    block = jnp.concatenate([(j > c), jnp.ones((HALF, HALF), bool)], axis=1)
    return jnp.concatenate([block, block], axis=0).astype(BF16)


def _attention(q, k, v):
    B, S, _ = q.shape
    n_pairs = ATTN_WIDTH // LANES
    q_spec = pl.BlockSpec((None, ATTN_ROWS, LANES), lambda b, p, s: (b, s, p))
    kv_spec = pl.BlockSpec((None, S, LANES), lambda b, p, s: (b, 0, p))
    return pl.pallas_call(
        _attn_kernel,
        grid=(B, n_pairs, S // ATTN_ROWS),
        in_specs=[q_spec, kv_spec, kv_spec, pl.BlockSpec((WIN, WIN), lambda b, p, s: (0, 0))],
        out_specs=q_spec,
        out_shape=jax.ShapeDtypeStruct((B, S, ATTN_WIDTH), BF16),
        scratch_shapes=[pltpu.VMEM((2 * SUB, LANES), F32), pltpu.VMEM((2 * SUB, LANES), F32)],
        compiler_params=pltpu.CompilerParams(
            dimension_semantics=("arbitrary", "arbitrary", "arbitrary"), vmem_limit_bytes=VMEM_LIMIT),
        name="stickbreak_attn",
    )(q, k, v, _suffix_sum_weights())


def _merge_kernel(x_ref, g_ref, w_ref, b_ref, yp_ref, at_ref, wpu_ref, wau_ref, wo_ref, fg_ref,
                  o_ref, *, final):
    x = x_ref[...]
    h = _rms_norm(x, g_ref[...]).astype(BF16)
    aw, d = ATTN_WIDTH, D_MODEL
    z_attn = _dot(h, w_ref[:, 0:aw])
    y_attn = (at_ref[...].astype(F32) * _silu(z_attn)).astype(BF16)
    gate_pool = jax.nn.sigmoid(_dot(h, w_ref[:, aw:aw + d]) + b_ref[:, 0:d])
    gate_attn = jax.nn.sigmoid(_dot(h, w_ref[:, aw + d:aw + 2 * d]) + b_ref[:, d:2 * d])
    merged = gate_pool * _dot(yp_ref[...], wpu_ref[...]) + gate_attn * _dot(y_attn, wau_ref[...])
    out = x + _dot(merged.astype(BF16), wo_ref[...])
    if final:
        out = _rms_norm(out, fg_ref[...])
    o_ref[...] = out


def _merge(x, g, w_c, b_gate, y_pool, attn, w_pool_up, w_attn_up, w_out, final_g, final):
    B, S, D = x.shape
    rows = MERGE_ROWS
    row_spec = lambda width: pl.BlockSpec((None, rows, width), lambda b, s: (b, s, 0))
    full = lambda shape: pl.BlockSpec(shape, lambda b, s: (0,) * len(shape))
    return pl.pallas_call(
        functools.partial(_merge_kernel, final=final),
        grid=(B, S // rows),
        in_specs=[row_spec(D), full((1, D)), full(w_c.shape), full((1, 2 * D)),
                  row_spec(POOL_WIDTH), row_spec(ATTN_WIDTH),
                  full(w_pool_up.shape), full(w_attn_up.shape), full(w_out.shape), full((1, D))],
        out_specs=row_spec(D),
        out_shape=jax.ShapeDtypeStruct((B, S, D), F32),
        compiler_params=pltpu.CompilerParams(
            dimension_semantics=("arbitrary", "arbitrary"), vmem_limit_bytes=VMEM_LIMIT),
        name="merge_out",
    )(x, g, w_c, b_gate, y_pool, attn, w_pool_up, w_attn_up, w_out, final_g)


def kernel(x, norm_g, w_in, b_gate, pool_w, pool_scale, w_pool_up, w_attn_up, w_out, final_g):
    depth = norm_g.shape[0]
    n_a = 2 * POOL_WIDTH + 3 * ATTN_WIDTH
    q_lo, q_hi = 2 * POOL_WIDTH, 2 * POOL_WIDTH + ATTN_WIDTH
    col_scale = jnp.ones((n_a,), F32).at[q_lo:q_hi].set(HEAD_DIM ** -0.5)
    final_g2 = final_g.reshape(1, D_MODEL)
    for l in range(depth):
        w_a = (w_in[l, :, :n_a] * col_scale).astype(BF16)
        w_c = w_in[l, :, n_a:].astype(BF16)
        g = norm_g[l].reshape(1, D_MODEL)
        y_pool, q, k, v = _proj_pool(x, g, w_a, pool_w[l].astype(BF16), pool_scale[l].reshape(1, POOL_WIDTH))
        attn = _attention(q, k, v)
        x = _merge(x, g, w_c, b_gate[l].reshape(1, 2 * D_MODEL), y_pool, attn,
                   w_pool_up[l].astype(BF16), w_attn_up[l].astype(BF16), w_out[l].astype(BF16),
                   final_g2, final=(l == depth - 1))
    return x
```

```python
import functools

import jax
import jax.numpy as jnp
from jax import lax
from jax.experimental import pallas as pl
from jax.experimental.pallas import tpu as pltpu

D_MODEL = 1024
POOL_WIDTH = 512
POOL_WINDOWS = (2, 4, 8, 16)
POOL_GROUP = 128
N_HEADS = 8
HEAD_DIM = 64
ATTN_WIDTH = N_HEADS * HEAD_DIM
RMS_EPS = 1e-6

LANES = 128
HALO = 16
PROJ_ROWS = 512
MERGE_ROWS = 256
ATTN_ROWS = 512
SUB = 64
WIN = 256
HALF = WIN // 2
GROUP = 8
LOG_WEIGHT_FLOOR = -110.0
MASKED_LOGIT = -1e30
VMEM_LIMIT = 48 * 1024 * 1024

F32 = jnp.float32
BF16 = jnp.bfloat16


def _rms_norm(x, g):
    ms = jnp.mean(x * x, axis=-1, keepdims=True)
    return x * lax.rsqrt(ms + RMS_EPS) * g


def _silu(z):
    return z * jax.nn.sigmoid(z)


def _dot(a, b):
    return jnp.dot(a, b, preferred_element_type=F32)


def _proj_pool_kernel(x_ref, g_ref, w_ref, pw_ref, ps_ref,
                      yp_ref, q_ref, k_ref, v_ref, ext_ref):
    s = pl.program_id(1)
    rows = x_ref.shape[0]
    h = _rms_norm(x_ref[...], g_ref[...]).astype(BF16)

    pw, aw = POOL_WIDTH, ATTN_WIDTH
    u = _dot(h, w_ref[:, 0:pw])
    z = _dot(h, w_ref[:, pw:2 * pw])
    q_ref[...] = _dot(h, w_ref[:, 2 * pw:2 * pw + aw]).astype(BF16)
    k_ref[...] = _dot(h, w_ref[:, 2 * pw + aw:2 * pw + 2 * aw]).astype(BF16)
    v_ref[...] = _dot(h, w_ref[:, 2 * pw + 2 * aw:2 * pw + 3 * aw]).astype(BF16)

    @pl.when(s == 0)
    def _():
        ext_ref[0:HALO, :] = jnp.zeros((HALO, pw), F32)

    ext_ref[HALO:HALO + rows, :] = u
    pos = s * rows + lax.broadcasted_iota(jnp.int32, (rows, 1), 0)
    for g, w in enumerate(POOL_WINDOWS):
        cols = slice(g * POOL_GROUP, (g + 1) * POOL_GROUP)
        ug = u[:, cols]
        acc = ug
        for i in range(1, w):
            acc = acc + ext_ref[HALO - i:HALO - i + rows, cols]
        cnt = jnp.minimum(pos + 1, w).astype(F32)
        pooled = acc / cnt - ug
        mixed = _dot(pooled.astype(BF16), pw_ref[g])
        y = mixed * ps_ref[:, cols] * _silu(z[:, cols])
        yp_ref[:, cols] = y.astype(BF16)
    ext_ref[0:HALO, :] = ext_ref[rows:rows + HALO, :]


def _proj_pool(x, g, w_a, pool_w, pool_scale):
    B, S, D = x.shape
    rows = PROJ_ROWS
    n_cols = w_a.shape[1]
    act = jax.ShapeDtypeStruct((B, S, ATTN_WIDTH), BF16)
    row_spec = lambda width: pl.BlockSpec((None, rows, width), lambda b, s: (b, s, 0))
    full = lambda shape: pl.BlockSpec(shape, lambda b, s: (0,) * len(shape))
    return pl.pallas_call(
        _proj_pool_kernel,
        grid=(B, S // rows),
        in_specs=[row_spec(D), full((1, D)), full((D, n_cols)),
                  full((len(POOL_WINDOWS), POOL_GROUP, POOL_GROUP)), full((1, POOL_WIDTH))],
        out_specs=[row_spec(POOL_WIDTH), row_spec(ATTN_WIDTH), row_spec(ATTN_WIDTH), row_spec(ATTN_WIDTH)],
        out_shape=[jax.ShapeDtypeStruct((B, S, POOL_WIDTH), BF16), act, act, act],
        scratch_shapes=[pltpu.VMEM((rows + HALO, POOL_WIDTH), F32)],
        compiler_params=pltpu.CompilerParams(
            dimension_semantics=("arbitrary", "arbitrary"), vmem_limit_bytes=VMEM_LIMIT),
        name="proj_pool",
    )(x, g, w_a, pool_w, pool_scale)


def _stack_heads(q2):
    lane = lax.broadcasted_iota(jnp.int32, q2.shape, 1)
    zero = jnp.zeros_like(q2)
    return jnp.concatenate([jnp.where(lane < HEAD_DIM, q2, zero),
                            jnp.where(lane >= HEAD_DIM, q2, zero)], axis=0)


def _windows(blocks, sufw):
    logits = [lax.dot_general(qs, k2, (((1,), (1,)), ((), ())), preferred_element_type=F32)
              for qs, k2, _, _, _, _ in blocks]
    log_betas, xs = [], []
    for lg, (_, _, _, bias_a, bias_b, _) in zip(logits, blocks):
        la, lb = lg[:, :HALF], lg[:, HALF:]
        if bias_a is not None:
            la = la + bias_a
        if bias_b is not None:
            lb = lb + bias_b
        lg = jnp.concatenate([la, lb], axis=1)
        softplus = jnp.log(1.0 + jnp.exp(-jnp.abs(lg)))
        log_beta = jnp.minimum(lg, 0.0) - softplus
        l1m = log_beta - lg
        l1m_hi = l1m.astype(BF16)
        l1m_lo = (l1m - l1m_hi.astype(F32)).astype(BF16)
        xs.append(jnp.concatenate(
            [jnp.concatenate([l1m_hi[:, :HALF], l1m_lo[:, :HALF]], axis=1),
             jnp.concatenate([l1m_hi[:, HALF:], l1m_lo[:, HALF:]], axis=1)], axis=0))
        log_betas.append(log_beta)
    sums = [_dot(x, sufw) for x in xs]
    ps, carries = [], []
    for r, log_beta, (_, _, _, _, _, carry) in zip(sums, log_betas, blocks):
        suf_a, tot_a = r[:2 * SUB, :HALF], r[:2 * SUB, HALF:]
        suf_b, tot_b = r[2 * SUB:, :HALF], r[2 * SUB:, HALF:]
        if carry is None:
            later = jnp.concatenate([suf_a + tot_b, suf_b], axis=1)
            carries.append(tot_a + tot_b)
        else:
            later = jnp.concatenate([suf_a + (tot_b + carry), suf_b + carry], axis=1)
            carries.append(carry + (tot_a + tot_b))
        ps.append(jnp.exp(log_beta + later).astype(BF16))
    return [(_dot(p, blk[2]), c) for p, c, blk in zip(ps, carries, blocks)]


def _attn_kernel(q_ref, k_ref, v_ref, sufw_ref, o_ref, acc_ref, carry_ref):
    step = pl.program_id(2)
    row_in_sub = lax.broadcasted_iota(jnp.int32, (2 * SUB, HALF), 0) & (SUB - 1)
    col_a = lax.broadcasted_iota(jnp.int32, (2 * SUB, HALF), 1)
    col_b = col_a + HALF
    lane = lax.broadcasted_iota(jnp.int32, (SUB, LANES), 1)

    def masked(key_offset_limit):
        return (jnp.where(col_a < key_offset_limit, 0.0, MASKED_LOGIT),
                jnp.where(col_b < key_offset_limit, 0.0, MASKED_LOGIT))

    _, diag_bias = masked((WIN - SUB) + row_in_sub)

    def trip(it, _):
        base_row = pl.multiple_of(it * (GROUP * SUB), GROUP * SUB)
        base_q = step * ATTN_ROWS + base_row
        at_sequence_start = base_q == 0

        def q_rows(g):
            return _stack_heads(q_ref[pl.ds(pl.multiple_of(base_row + g * SUB, SUB), SUB), :])

        def first_windows(blocks):
            for g, (pv, tot) in enumerate(_windows(blocks, sufw_ref[...])):
                acc_ref[g] = pv
                carry_ref[g] = tot

        @pl.when(at_sequence_start)
        def _():
            blocks = []
            for g in range(GROUP):
                start = max(g * SUB - (WIN - SUB), 0)
                blocks.append((q_rows(g), k_ref[start:start + WIN, :], v_ref[start:start + WIN, :])
                              + masked(g * SUB - start + row_in_sub) + (None,))
            first_windows(blocks)

        @pl.when(jnp.logical_not(at_sequence_start))
        def _():
            blocks = []
            for g in range(GROUP):
                start = pl.multiple_of(base_q + g * SUB - (WIN - SUB), SUB)
                blocks.append((q_rows(g), k_ref[pl.ds(start, WIN), :], v_ref[pl.ds(start, WIN), :],
                               None, diag_bias, None))
            first_windows(blocks)

        worst = carry_ref[0]
        for g in range(1, GROUP):
            worst = jnp.maximum(worst, carry_ref[g])

        @pl.when(jnp.max(worst) > LOG_WEIGHT_FLOOR)
        def _():
            def sweep_block(g, _):
                qs = q_rows(g)

                def cond(c):
                    start, m = c
                    return jnp.logical_and(start > 0, m > LOG_WEIGHT_FLOOR)

                def body(c):
                    start, _ = c
                    nxt = pl.multiple_of(jnp.maximum(start - WIN, 0), SUB)
                    block = (qs, k_ref[pl.ds(nxt, WIN), :], v_ref[pl.ds(nxt, WIN), :]) + masked(start - nxt)
                    (pv, tot), = _windows([block + (carry_ref[g],)], sufw_ref[...])
                    acc_ref[g] = acc_ref[g] + pv
                    carry_ref[g] = tot
                    return nxt, jnp.max(tot)

                start0 = jnp.maximum(base_q + g * SUB - (WIN - SUB), 0)
                lax.while_loop(cond, body, (start0, jnp.max(carry_ref[g])))
                return 0

            lax.fori_loop(0, GROUP, sweep_block, 0)

        for g in range(GROUP):
            acc = acc_ref[g]
            o_ref[pl.ds(pl.multiple_of(base_row + g * SUB, SUB), SUB), :] = (
                jnp.where(lane < HEAD_DIM, acc[:SUB], acc[SUB:]).astype(BF16))
        return 0

    lax.fori_loop(0, ATTN_ROWS // (GROUP * SUB), trip, 0)


def _suffix_sum_weights():
    j = jnp.arange(HALF)[:, None]
    c = jnp.arange(HALF)[None, :]
    block = jnp.concatenate([(j > c), jnp.ones((HALF, HALF), bool)], axis=1)
    return jnp.concatenate([block, block], axis=0).astype(BF16)


def _attention(q, k, v):
    B, S, _ = q.shape
    n_pairs = ATTN_WIDTH // LANES
    q_spec = pl.BlockSpec((None, ATTN_ROWS, LANES), lambda b, p, s: (b, s, p))
    kv_spec = pl.BlockSpec((None, S, LANES), lambda b, p, s: (b, 0, p))
    return pl.pallas_call(
        _attn_kernel,
        grid=(B, n_pairs, S // ATTN_ROWS),
        in_specs=[q_spec, kv_spec, kv_spec, pl.BlockSpec((WIN, WIN), lambda b, p, s: (0, 0))],
        out_specs=q_spec,
        out_shape=jax.ShapeDtypeStruct((B, S, ATTN_WIDTH), BF16),
        scratch_shapes=[pltpu.VMEM((GROUP, 2 * SUB, LANES), F32), pltpu.VMEM((GROUP, 2 * SUB, LANES), F32)],
        compiler_params=pltpu.CompilerParams(
            dimension_semantics=("arbitrary", "arbitrary", "arbitrary"), vmem_limit_bytes=VMEM_LIMIT),
        name="stickbreak_attn",
    )(q, k, v, _suffix_sum_weights())


def _merge_kernel(x_ref, g_ref, w_ref, b_ref, yp_ref, at_ref, wpu_ref, wau_ref, wo_ref, fg_ref,
                  o_ref, *, final):
    x = x_ref[...]
    h = _rms_norm(x, g_ref[...]).astype(BF16)
    aw, d = ATTN_WIDTH, D_MODEL
    z_attn = _dot(h, w_ref[:, 0:aw])
    y_attn = (at_ref[...].astype(F32) * _silu(z_attn)).astype(BF16)
    gate_pool = jax.nn.sigmoid(_dot(h, w_ref[:, aw:aw + d]) + b_ref[:, 0:d])
    gate_attn = jax.nn.sigmoid(_dot(h, w_ref[:, aw + d:aw + 2 * d]) + b_ref[:, d:2 * d])
    merged = gate_pool * _dot(yp_ref[...], wpu_ref[...]) + gate_attn * _dot(y_attn, wau_ref[...])
    out = x + _dot(merged.astype(BF16), wo_ref[...])
    if final:
        out = _rms_norm(out, fg_ref[...])
    o_ref[...] = out


def _merge(x, g, w_c, b_gate, y_pool, attn, w_pool_up, w_attn_up, w_out, final_g, final):
    B, S, D = x.shape
    rows = MERGE_ROWS
    row_spec = lambda width: pl.BlockSpec((None, rows, width), lambda b, s: (b, s, 0))
    full = lambda shape: pl.BlockSpec(shape, lambda b, s: (0,) * len(shape))
    return pl.pallas_call(
        functools.partial(_merge_kernel, final=final),
        grid=(B, S // rows),
        in_specs=[row_spec(D), full((1, D)), full(w_c.shape), full((1, 2 * D)),
                  row_spec(POOL_WIDTH), row_spec(ATTN_WIDTH),
                  full(w_pool_up.shape), full(w_attn_up.shape), full(w_out.shape), full((1, D))],
        out_specs=row_spec(D),
        out_shape=jax.ShapeDtypeStruct((B, S, D), F32),
        compiler_params=pltpu.CompilerParams(
            dimension_semantics=("arbitrary", "arbitrary"), vmem_limit_bytes=VMEM_LIMIT),
        name="merge_out",
    )(x, g, w_c, b_gate, y_pool, attn, w_pool_up, w_attn_up, w_out, final_g)


def kernel(x, norm_g, w_in, b_gate, pool_w, pool_scale, w_pool_up, w_attn_up, w_out, final_g):
    depth = norm_g.shape[0]
    n_a = 2 * POOL_WIDTH + 3 * ATTN_WIDTH
    q_lo, q_hi = 2 * POOL_WIDTH, 2 * POOL_WIDTH + ATTN_WIDTH
    col_scale = jnp.ones((n_a,), F32).at[q_lo:q_hi].set(HEAD_DIM ** -0.5)
    final_g2 = final_g.reshape(1, D_MODEL)
    for l in range(depth):
        w_a = (w_in[l, :, :n_a] * col_scale).astype(BF16)
        w_c = w_in[l, :, n_a:].astype(BF16)
        g = norm_g[l].reshape(1, D_MODEL)
        y_pool, q, k, v = _proj_pool(x, g, w_a, pool_w[l].astype(BF16), pool_scale[l].reshape(1, POOL_WIDTH))
        attn = _attention(q, k, v)
        x = _merge(x, g, w_c, b_gate[l].reshape(1, 2 * D_MODEL), y_pool, attn,
                   w_pool_up[l].astype(BF16), w_attn_up[l].astype(BF16), w_out[l].astype(BF16),
                   final_g2, final=(l == depth - 1))
    return x
```

```python
import functools

import jax
import jax.numpy as jnp
from jax import lax
from jax.experimental import pallas as pl
from jax.experimental.pallas import tpu as pltpu

D_MODEL = 1024
POOL_WIDTH = 512
POOL_WINDOWS = (2, 4, 8, 16)
POOL_GROUP = 128
N_HEADS = 8
HEAD_DIM = 64
ATTN_WIDTH = N_HEADS * HEAD_DIM
RMS_EPS = 1e-6

LANES = 128
HALO = 16
PROJ_ROWS = 1024
MERGE_ROWS = 512
ATTN_ROWS = 1024
SUB = 64
WIN = 256
HALF = WIN // 2
GROUP = 8
LOG_WEIGHT_FLOOR = -110.0
MASKED_LOGIT = -1e30
VMEM_LIMIT = 48 * 1024 * 1024

F32 = jnp.float32
BF16 = jnp.bfloat16


def _rms_norm(x, g):
    ms = jnp.mean(x * x, axis=-1, keepdims=True)
    return x * lax.rsqrt(ms + RMS_EPS) * g


def _silu(z):
    return z * jax.nn.sigmoid(z)


def _dot(a, b):
    return jnp.dot(a, b, preferred_element_type=F32)


def _proj_pool_kernel(x_ref, g_ref, w_ref, pw_ref, ps_ref,
                      yp_ref, q_ref, k_ref, v_ref, ext_ref):
    s = pl.program_id(1)
    rows = x_ref.shape[0]
    h = _rms_norm(x_ref[...], g_ref[...]).astype(BF16)

    pw, aw = POOL_WIDTH, ATTN_WIDTH
    u = _dot(h, w_ref[:, 0:pw])
    z = _dot(h, w_ref[:, pw:2 * pw])
    q_ref[...] = _dot(h, w_ref[:, 2 * pw:2 * pw + aw]).astype(BF16)
    k_ref[...] = _dot(h, w_ref[:, 2 * pw + aw:2 * pw + 2 * aw]).astype(BF16)
    v_ref[...] = _dot(h, w_ref[:, 2 * pw + 2 * aw:2 * pw + 3 * aw]).astype(BF16)

    @pl.when(s == 0)
    def _():
        ext_ref[0:HALO, :] = jnp.zeros((HALO, pw), F32)

    ext_ref[HALO:HALO + rows, :] = u
    pos = s * rows + lax.broadcasted_iota(jnp.int32, (rows, 1), 0)
    for g, w in enumerate(POOL_WINDOWS):
        cols = slice(g * POOL_GROUP, (g + 1) * POOL_GROUP)
        ug = u[:, cols]
        acc = ug
        for i in range(1, w):
            acc = acc + ext_ref[HALO - i:HALO - i + rows, cols]
        cnt = jnp.minimum(pos + 1, w).astype(F32)
        pooled = acc / cnt - ug
        mixed = _dot(pooled.astype(BF16), pw_ref[g])
        y = mixed * ps_ref[:, cols] * _silu(z[:, cols])
        yp_ref[:, cols] = y.astype(BF16)
    ext_ref[0:HALO, :] = ext_ref[rows:rows + HALO, :]


def _proj_pool(x, g, w_a, pool_w, pool_scale):
    B, S, D = x.shape
    rows = PROJ_ROWS
    n_cols = w_a.shape[1]
    act = jax.ShapeDtypeStruct((B, S, ATTN_WIDTH), BF16)
    row_spec = lambda width: pl.BlockSpec((None, rows, width), lambda b, s: (b, s, 0))
    full = lambda shape: pl.BlockSpec(shape, lambda b, s: (0,) * len(shape), pipeline_mode=pl.Buffered(1))
    return pl.pallas_call(
        _proj_pool_kernel,
        grid=(B, S // rows),
        in_specs=[row_spec(D), full((1, D)), full((D, n_cols)),
                  full((len(POOL_WINDOWS), POOL_GROUP, POOL_GROUP)), full((1, POOL_WIDTH))],
        out_specs=[row_spec(POOL_WIDTH), row_spec(ATTN_WIDTH), row_spec(ATTN_WIDTH), row_spec(ATTN_WIDTH)],
        out_shape=[jax.ShapeDtypeStruct((B, S, POOL_WIDTH), BF16), act, act, act],
        scratch_shapes=[pltpu.VMEM((rows + HALO, POOL_WIDTH), F32)],
        compiler_params=pltpu.CompilerParams(
            dimension_semantics=("arbitrary", "arbitrary"), vmem_limit_bytes=VMEM_LIMIT),
        name="proj_pool",
    )(x, g, w_a, pool_w, pool_scale)


def _stack_heads(q2):
    lane = lax.broadcasted_iota(jnp.int32, q2.shape, 1)
    zero = jnp.zeros_like(q2)
    return jnp.concatenate([jnp.where(lane < HEAD_DIM, q2, zero),
                            jnp.where(lane >= HEAD_DIM, q2, zero)], axis=0)


def _windows(blocks, sufw):
    logits = [lax.dot_general(qs, k2, (((1,), (1,)), ((), ())), preferred_element_type=F32)
              for qs, k2, _, _, _, _ in blocks]
    log_betas, xs = [], []
    for lg, (_, _, _, bias_a, bias_b, _) in zip(logits, blocks):
        la, lb = lg[:, :HALF], lg[:, HALF:]
        if bias_a is not None:
            la = la + bias_a
        if bias_b is not None:
            lb = lb + bias_b
        lg = jnp.concatenate([la, lb], axis=1)
        softplus = jnp.log(1.0 + jnp.exp(-jnp.abs(lg)))
        log_beta = jnp.minimum(lg, 0.0) - softplus
        l1m = log_beta - lg
        l1m_hi = l1m.astype(BF16)
        l1m_lo = (l1m - l1m_hi.astype(F32)).astype(BF16)
        xs.append(jnp.concatenate(
            [jnp.concatenate([l1m_hi[:, :HALF], l1m_lo[:, :HALF]], axis=1),
             jnp.concatenate([l1m_hi[:, HALF:], l1m_lo[:, HALF:]], axis=1)], axis=0))
        log_betas.append(log_beta)
    sums = [_dot(x, sufw) for x in xs]
    ps, carries = [], []
    for r, log_beta, (_, _, _, _, _, carry) in zip(sums, log_betas, blocks):
        suf_a, tot_a = r[:2 * SUB, :HALF], r[:2 * SUB, HALF:]
        suf_b, tot_b = r[2 * SUB:, :HALF], r[2 * SUB:, HALF:]
        if carry is None:
            later = jnp.concatenate([suf_a + tot_b, suf_b], axis=1)
            carries.append(tot_a + tot_b)
        else:
            later = jnp.concatenate([suf_a + (tot_b + carry), suf_b + carry], axis=1)
            carries.append(carry + (tot_a + tot_b))
        ps.append(jnp.exp(log_beta + later).astype(BF16))
    return [(_dot(p, blk[2]), c) for p, c, blk in zip(ps, carries, blocks)]


def _attn_kernel(q_ref, k_ref, v_ref, sufw_ref, o_ref, acc_ref, carry_ref, worst_ref):
    step = pl.program_id(2)
    row_in_sub = lax.broadcasted_iota(jnp.int32, (2 * SUB, HALF), 0) & (SUB - 1)
    col_a = lax.broadcasted_iota(jnp.int32, (2 * SUB, HALF), 1)
    col_b = col_a + HALF
    lane = lax.broadcasted_iota(jnp.int32, (SUB, LANES), 1)

    def masked(key_offset_limit):
        return (jnp.where(col_a < key_offset_limit, 0.0, MASKED_LOGIT),
                jnp.where(col_b < key_offset_limit, 0.0, MASKED_LOGIT))

    _, diag_bias = masked((WIN - SUB) + row_in_sub)

    def trip(it, _):
        base_row = pl.multiple_of(it * (GROUP * SUB), GROUP * SUB)
        base_q = step * ATTN_ROWS + base_row
        at_sequence_start = base_q == 0

        def q_rows(g):
            return _stack_heads(q_ref[pl.ds(pl.multiple_of(base_row + g * SUB, SUB), SUB), :])

        def write_rows(g, acc):
            o_ref[pl.ds(pl.multiple_of(base_row + g * SUB, SUB), SUB), :] = (
                jnp.where(lane < HEAD_DIM, acc[:SUB], acc[SUB:]).astype(BF16))

        def first_windows(blocks):
            worst = None
            for g, (pv, tot) in enumerate(_windows(blocks, sufw_ref[...])):
                acc_ref[g] = pv
                carry_ref[g] = tot
                write_rows(g, pv)
                worst = tot if worst is None else jnp.maximum(worst, tot)
            worst_ref[0] = jnp.max(worst)

        @pl.when(at_sequence_start)
        def _():
            blocks = []
            for g in range(GROUP):
                start = max(g * SUB - (WIN - SUB), 0)
                blocks.append((q_rows(g), k_ref[start:start + WIN, :], v_ref[start:start + WIN, :])
                              + masked(g * SUB - start + row_in_sub) + (None,))
            first_windows(blocks)

        @pl.when(jnp.logical_not(at_sequence_start))
        def _():
            blocks = []
            for g in range(GROUP):
                start = pl.multiple_of(base_q + g * SUB - (WIN - SUB), SUB)
                blocks.append((q_rows(g), k_ref[pl.ds(start, WIN), :], v_ref[pl.ds(start, WIN), :],
                               None, diag_bias, None))
            first_windows(blocks)

        @pl.when(worst_ref[0] > LOG_WEIGHT_FLOOR)
        def _():
            def sweep_block(g, _):
                qs = q_rows(g)

                def cond(c):
                    start, m = c
                    return jnp.logical_and(start > 0, m > LOG_WEIGHT_FLOOR)

                def body(c):
                    start, _ = c
                    nxt = pl.multiple_of(jnp.maximum(start - WIN, 0), SUB)
                    block = (qs, k_ref[pl.ds(nxt, WIN), :], v_ref[pl.ds(nxt, WIN), :]) + masked(start - nxt)
                    (pv, tot), = _windows([block + (carry_ref[g],)], sufw_ref[...])
                    acc_ref[g] = acc_ref[g] + pv
                    carry_ref[g] = tot
                    return nxt, jnp.max(tot)

                start0 = jnp.maximum(base_q + g * SUB - (WIN - SUB), 0)
                lax.while_loop(cond, body, (start0, jnp.max(carry_ref[g])))
                write_rows(g, acc_ref[g])
                return 0

            lax.fori_loop(0, GROUP, sweep_block, 0)

        return 0

    lax.fori_loop(0, ATTN_ROWS // (GROUP * SUB), trip, 0)


def _suffix_sum_weights():
    j = jnp.arange(HALF)[:, None]
    c = jnp.arange(HALF)[None, :]
    block = jnp.concatenate([(j > c), jnp.ones((HALF, HALF), bool)], axis=1)
    return jnp.concatenate([block, block], axis=0).astype(BF16)


def _attention(q, k, v):
    B, S, _ = q.shape
    n_pairs = ATTN_WIDTH // LANES
    q_spec = pl.BlockSpec((None, ATTN_ROWS, LANES), lambda b, p, s: (b, s, p))
    kv_spec = pl.BlockSpec((None, S, LANES), lambda b, p, s: (b, 0, p))
    return pl.pallas_call(
        _attn_kernel,
        grid=(B, n_pairs, S // ATTN_ROWS),
        in_specs=[q_spec, kv_spec, kv_spec, pl.BlockSpec((WIN, WIN), lambda b, p, s: (0, 0))],
        out_specs=q_spec,
        out_shape=jax.ShapeDtypeStruct((B, S, ATTN_WIDTH), BF16),
        scratch_shapes=[pltpu.VMEM((GROUP, 2 * SUB, LANES), F32), pltpu.VMEM((GROUP, 2 * SUB, LANES), F32),
                        pltpu.SMEM((1,), F32)],
        compiler_params=pltpu.CompilerParams(
            dimension_semantics=("arbitrary", "arbitrary", "arbitrary"), vmem_limit_bytes=VMEM_LIMIT),
        name="stickbreak_attn",
    )(q, k, v, _suffix_sum_weights())


def _merge_kernel(x_ref, g_ref, w_ref, b_ref, yp_ref, at_ref, wpu_ref, wau_ref, wo_ref, fg_ref,
                  o_ref, *, final):
    x = x_ref[...]
    h = _rms_norm(x, g_ref[...]).astype(BF16)
    aw, d = ATTN_WIDTH, D_MODEL
    z_attn = _dot(h, w_ref[:, 0:aw])
    y_attn = (at_ref[...].astype(F32) * _silu(z_attn)).astype(BF16)
    gate_pool = jax.nn.sigmoid(_dot(h, w_ref[:, aw:aw + d]) + b_ref[:, 0:d])
    gate_attn = jax.nn.sigmoid(_dot(h, w_ref[:, aw + d:aw + 2 * d]) + b_ref[:, d:2 * d])
    merged = gate_pool * _dot(yp_ref[...], wpu_ref[...]) + gate_attn * _dot(y_attn, wau_ref[...])
    out = x + _dot(merged.astype(BF16), wo_ref[...])
    if final:
        out = _rms_norm(out, fg_ref[...])
    o_ref[...] = out


def _merge(x, g, w_c, b_gate, y_pool, attn, w_pool_up, w_attn_up, w_out, final_g, final):
    B, S, D = x.shape
    rows = MERGE_ROWS
    row_spec = lambda width: pl.BlockSpec((None, rows, width), lambda b, s: (b, s, 0))
    full = lambda shape: pl.BlockSpec(shape, lambda b, s: (0,) * len(shape), pipeline_mode=pl.Buffered(1))
    return pl.pallas_call(
        functools.partial(_merge_kernel, final=final),
        grid=(B, S // rows),
        in_specs=[row_spec(D), full((1, D)), full(w_c.shape), full((1, 2 * D)),
                  row_spec(POOL_WIDTH), row_spec(ATTN_WIDTH),
                  full(w_pool_up.shape), full(w_attn_up.shape), full(w_out.shape), full((1, D))],
        out_specs=row_spec(D),
        out_shape=jax.ShapeDtypeStruct((B, S, D), F32),
        compiler_params=pltpu.CompilerParams(
            dimension_semantics=("arbitrary", "arbitrary"), vmem_limit_bytes=VMEM_LIMIT),
        name="merge_out",
    )(x, g, w_c, b_gate, y_pool, attn, w_pool_up, w_attn_up, w_out, final_g)


def kernel(x, norm_g, w_in, b_gate, pool_w, pool_scale, w_pool_up, w_attn_up, w_out, final_g):
    depth = norm_g.shape[0]
    n_a = 2 * POOL_WIDTH + 3 * ATTN_WIDTH
    q_lo, q_hi = 2 * POOL_WIDTH, 2 * POOL_WIDTH + ATTN_WIDTH
    col_scale = jnp.ones((n_a,), F32).at[q_lo:q_hi].set(HEAD_DIM ** -0.5)
    final_g2 = final_g.reshape(1, D_MODEL)
    for l in range(depth):
        w_a = (w_in[l, :, :n_a] * col_scale).astype(BF16)
        w_c = w_in[l, :, n_a:].astype(BF16)
        g = norm_g[l].reshape(1, D_MODEL)
        y_pool, q, k, v = _proj_pool(x, g, w_a, pool_w[l].astype(BF16), pool_scale[l].reshape(1, POOL_WIDTH))
        attn = _attention(q, k, v)
        x = _merge(x, g, w_c, b_gate[l].reshape(1, 2 * D_MODEL), y_pool, attn,
                   w_pool_up[l].astype(BF16), w_attn_up[l].astype(BF16), w_out[l].astype(BF16),
                   final_g2, final=(l == depth - 1))
    return x
```

```python
import functools

import jax
import jax.numpy as jnp
from jax import lax
from jax.experimental import pallas as pl
from jax.experimental.pallas import tpu as pltpu

D_MODEL = 1024
POOL_WIDTH = 512
POOL_WINDOWS = (2, 4, 8, 16)
POOL_GROUP = 128
N_HEADS = 8
HEAD_DIM = 64
ATTN_WIDTH = N_HEADS * HEAD_DIM
RMS_EPS = 1e-6

LANES = 128
HALO = 16
PROJ_ROWS = 1024
PROJ_SUB = 256
MERGE_ROWS = 512
MERGE_SUB = 256
ATTN_ROWS = 1024
SUB = 64
WIN = 256
HALF = WIN // 2
GROUP = 8
LOG_WEIGHT_FLOOR = -110.0
MASKED_LOGIT = -1e30
VMEM_LIMIT = 48 * 1024 * 1024

F32 = jnp.float32
BF16 = jnp.bfloat16


def _rms_norm(x, g):
    ms = jnp.mean(x * x, axis=-1, keepdims=True)
    return x * lax.rsqrt(ms + RMS_EPS) * g


def _silu(z):
    return z * jax.nn.sigmoid(z)


def _dot(a, b):
    return jnp.dot(a, b, preferred_element_type=F32)


def _proj_pool_kernel(x_ref, g_ref, w_ref, pw_ref, ps_ref,
                      yp_ref, q_ref, k_ref, v_ref, ext_ref):
    s = pl.program_id(1)
    rows = x_ref.shape[0]
    pw, aw = POOL_WIDTH, ATTN_WIDTH

    @pl.when(s == 0)
    def _():
        ext_ref[0:HALO, :] = jnp.zeros((HALO, pw), F32)

    for r0 in range(0, rows, PROJ_SUB):
        sub = slice(r0, r0 + PROJ_SUB)
        h = _rms_norm(x_ref[sub, :], g_ref[...]).astype(BF16)
        ext_ref[HALO + r0:HALO + r0 + PROJ_SUB, :] = _dot(h, w_ref[:, 0:pw])
        z = _dot(h, w_ref[:, pw:2 * pw])
        q_ref[sub, :] = _dot(h, w_ref[:, 2 * pw:2 * pw + aw]).astype(BF16)
        k_ref[sub, :] = _dot(h, w_ref[:, 2 * pw + aw:2 * pw + 2 * aw]).astype(BF16)
        v_ref[sub, :] = _dot(h, w_ref[:, 2 * pw + 2 * aw:2 * pw + 3 * aw]).astype(BF16)

        head_pos = s * rows + r0 + lax.broadcasted_iota(jnp.int32, (HALO, 1), 0)
        for g, w in enumerate(POOL_WINDOWS):
            cols = slice(g * POOL_GROUP, (g + 1) * POOL_GROUP)
            e = ext_ref[r0:r0 + HALO + PROJ_SUB, cols]
            acc, span = e, 1
            while span < w:
                acc = acc + pltpu.roll(acc, span, axis=0)
                span *= 2
            ug, acc = e[HALO:], acc[HALO:]
            cnt = jnp.minimum(head_pos + 1, w).astype(F32)
            pooled = jnp.concatenate([acc[:HALO] / cnt, acc[HALO:] * (1.0 / w)], axis=0) - ug
            mixed = _dot(pooled.astype(BF16), pw_ref[g])
            y = mixed * ps_ref[:, cols] * _silu(z[:, cols])
            yp_ref[sub, cols] = y.astype(BF16)
    ext_ref[0:HALO, :] = ext_ref[rows:rows + HALO, :]


def _proj_pool(x, g, w_a, pool_w, pool_scale):
    B, S, D = x.shape
    rows = PROJ_ROWS
    n_cols = w_a.shape[1]
    act = jax.ShapeDtypeStruct((B, S, ATTN_WIDTH), BF16)
    row_spec = lambda width: pl.BlockSpec((None, rows, width), lambda b, s: (b, s, 0))
    full = lambda shape: pl.BlockSpec(shape, lambda b, s: (0,) * len(shape), pipeline_mode=pl.Buffered(1))
    return pl.pallas_call(
        _proj_pool_kernel,
        grid=(B, S // rows),
        in_specs=[row_spec(D), full((1, D)), full((D, n_cols)),
                  full((len(POOL_WINDOWS), POOL_GROUP, POOL_GROUP)), full((1, POOL_WIDTH))],
        out_specs=[row_spec(POOL_WIDTH), row_spec(ATTN_WIDTH), row_spec(ATTN_WIDTH), row_spec(ATTN_WIDTH)],
        out_shape=[jax.ShapeDtypeStruct((B, S, POOL_WIDTH), BF16), act, act, act],
        scratch_shapes=[pltpu.VMEM((rows + HALO, POOL_WIDTH), F32)],
        compiler_params=pltpu.CompilerParams(
            dimension_semantics=("arbitrary", "arbitrary"), vmem_limit_bytes=VMEM_LIMIT),
        name="proj_pool",
    )(x, g, w_a, pool_w, pool_scale)


def _stack_heads(q2):
    lane = lax.broadcasted_iota(jnp.int32, q2.shape, 1)
    zero = jnp.zeros_like(q2)
    return jnp.concatenate([jnp.where(lane < HEAD_DIM, q2, zero),
                            jnp.where(lane >= HEAD_DIM, q2, zero)], axis=0)


def _windows(blocks, sufw):
    logits = [lax.dot_general(qs, k2, (((1,), (1,)), ((), ())), preferred_element_type=F32)
              for qs, k2, _, _, _, _ in blocks]
    log_betas, xs = [], []
    for lg, (_, _, _, bias_a, bias_b, _) in zip(logits, blocks):
        la, lb = lg[:, :HALF], lg[:, HALF:]
        if bias_a is not None:
            la = la + bias_a
        if bias_b is not None:
            lb = lb + bias_b
        lg = jnp.concatenate([la, lb], axis=1)
        softplus = jnp.log(1.0 + jnp.exp(-jnp.abs(lg)))
        log_beta = jnp.minimum(lg, 0.0) - softplus
        l1m = log_beta - lg
        l1m_hi = l1m.astype(BF16)
        l1m_lo = (l1m - l1m_hi.astype(F32)).astype(BF16)
        xs.append(jnp.concatenate(
            [jnp.concatenate([l1m_hi[:, :HALF], l1m_lo[:, :HALF]], axis=1),
             jnp.concatenate([l1m_hi[:, HALF:], l1m_lo[:, HALF:]], axis=1)], axis=0))
        log_betas.append(log_beta)
    sums = [_dot(x, sufw) for x in xs]
    ps, carries = [], []
    for r, log_beta, (_, _, _, _, _, carry) in zip(sums, log_betas, blocks):
        suf_a, tot_a = r[:2 * SUB, :HALF], r[:2 * SUB, HALF:]
        suf_b, tot_b = r[2 * SUB:, :HALF], r[2 * SUB:, HALF:]
        if carry is None:
            later = jnp.concatenate([suf_a + tot_b, suf_b], axis=1)
            carries.append(tot_a + tot_b)
        else:
            later = jnp.concatenate([suf_a + (tot_b + carry), suf_b + carry], axis=1)
            carries.append(carry + (tot_a + tot_b))
        ps.append(jnp.exp(log_beta + later).astype(BF16))
    return [(_dot(p, blk[2]), c) for p, c, blk in zip(ps, carries, blocks)]


def _attn_kernel(q_ref, k_ref, v_ref, sufw_ref, o_ref, acc_ref, carry_ref, worst_ref):
    step = pl.program_id(2)
    row_in_sub = lax.broadcasted_iota(jnp.int32, (2 * SUB, HALF), 0) & (SUB - 1)
    col_a = lax.broadcasted_iota(jnp.int32, (2 * SUB, HALF), 1)
    col_b = col_a + HALF
    lane = lax.broadcasted_iota(jnp.int32, (SUB, LANES), 1)

    def masked(key_offset_limit):
        return (jnp.where(col_a < key_offset_limit, 0.0, MASKED_LOGIT),
                jnp.where(col_b < key_offset_limit, 0.0, MASKED_LOGIT))

    _, diag_bias = masked((WIN - SUB) + row_in_sub)

    def trip(it, _):
        base_row = pl.multiple_of(it * (GROUP * SUB), GROUP * SUB)
        base_q = step * ATTN_ROWS + base_row
        at_sequence_start = base_q == 0

        def q_rows(g):
            return _stack_heads(q_ref[pl.ds(pl.multiple_of(base_row + g * SUB, SUB), SUB), :])

        def write_rows(g, acc):
            o_ref[pl.ds(pl.multiple_of(base_row + g * SUB, SUB), SUB), :] = (
                jnp.where(lane < HEAD_DIM, acc[:SUB], acc[SUB:]).astype(BF16))

        def first_windows(blocks):
            worst = None
            for g, (pv, tot) in enumerate(_windows(blocks, sufw_ref[...])):
                acc_ref[g] = pv
                carry_ref[g] = tot
                write_rows(g, pv)
                worst = tot if worst is None else jnp.maximum(worst, tot)
            worst_ref[0] = jnp.max(worst)

        @pl.when(at_sequence_start)
        def _():
            blocks = []
            for g in range(GROUP):
                start = max(g * SUB - (WIN - SUB), 0)
                blocks.append((q_rows(g), k_ref[start:start + WIN, :], v_ref[start:start + WIN, :])
                              + masked(g * SUB - start + row_in_sub) + (None,))
            first_windows(blocks)

        @pl.when(jnp.logical_not(at_sequence_start))
        def _():
            blocks = []
            for g in range(GROUP):
                start = pl.multiple_of(base_q + g * SUB - (WIN - SUB), SUB)
                blocks.append((q_rows(g), k_ref[pl.ds(start, WIN), :], v_ref[pl.ds(start, WIN), :],
                               None, diag_bias, None))
            first_windows(blocks)

        @pl.when(worst_ref[0] > LOG_WEIGHT_FLOOR)
        def _():
            def sweep_block(g, _):
                qs = q_rows(g)

                def cond(c):
                    start, m = c
                    return jnp.logical_and(start > 0, m > LOG_WEIGHT_FLOOR)

                def body(c):
                    start, _ = c
                    nxt = pl.multiple_of(jnp.maximum(start - WIN, 0), SUB)
                    block = (qs, k_ref[pl.ds(nxt, WIN), :], v_ref[pl.ds(nxt, WIN), :]) + masked(start - nxt)
                    (pv, tot), = _windows([block + (carry_ref[g],)], sufw_ref[...])
                    acc_ref[g] = acc_ref[g] + pv
                    carry_ref[g] = tot
                    return nxt, jnp.max(tot)

                start0 = jnp.maximum(base_q + g * SUB - (WIN - SUB), 0)
                lax.while_loop(cond, body, (start0, jnp.max(carry_ref[g])))
                write_rows(g, acc_ref[g])
                return 0

            lax.fori_loop(0, GROUP, sweep_block, 0)

        return 0

    lax.fori_loop(0, ATTN_ROWS // (GROUP * SUB), trip, 0)


def _suffix_sum_weights():
    j = jnp.arange(HALF)[:, None]
    c = jnp.arange(HALF)[None, :]
    block = jnp.concatenate([(j > c), jnp.ones((HALF, HALF), bool)], axis=1)
    return jnp.concatenate([block, block], axis=0).astype(BF16)


def _attention(q, k, v):
    B, S, _ = q.shape
    n_pairs = ATTN_WIDTH // LANES
    q_spec = pl.BlockSpec((None, ATTN_ROWS, LANES), lambda b, p, s: (b, s, p))
    kv_spec = pl.BlockSpec((None, S, LANES), lambda b, p, s: (b, 0, p))
    return pl.pallas_call(
        _attn_kernel,
        grid=(B, n_pairs, S // ATTN_ROWS),
        in_specs=[q_spec, kv_spec, kv_spec, pl.BlockSpec((WIN, WIN), lambda b, p, s: (0, 0))],
        out_specs=q_spec,
        out_shape=jax.ShapeDtypeStruct((B, S, ATTN_WIDTH), BF16),
        scratch_shapes=[pltpu.VMEM((GROUP, 2 * SUB, LANES), F32), pltpu.VMEM((GROUP, 2 * SUB, LANES), F32),
                        pltpu.SMEM((1,), F32)],
        compiler_params=pltpu.CompilerParams(
            dimension_semantics=("arbitrary", "arbitrary", "arbitrary"), vmem_limit_bytes=VMEM_LIMIT),
        name="stickbreak_attn",
    )(q, k, v, _suffix_sum_weights())


def _merge_kernel(x_ref, g_ref, w_ref, b_ref, yp_ref, at_ref, wpu_ref, wau_ref, wo_ref, fg_ref,
                  o_ref, *, final):
    aw, d = ATTN_WIDTH, D_MODEL
    for r0 in range(0, x_ref.shape[0], MERGE_SUB):
        sub = slice(r0, r0 + MERGE_SUB)
        x = x_ref[sub, :]
        h = _rms_norm(x, g_ref[...]).astype(BF16)
        z_attn = _dot(h, w_ref[:, 0:aw])
        y_attn = (at_ref[sub, :].astype(F32) * _silu(z_attn)).astype(BF16)
        gate_pool = jax.nn.sigmoid(_dot(h, w_ref[:, aw:aw + d]) + b_ref[:, 0:d])
        gate_attn = jax.nn.sigmoid(_dot(h, w_ref[:, aw + d:aw + 2 * d]) + b_ref[:, d:2 * d])
        merged = gate_pool * _dot(yp_ref[sub, :], wpu_ref[...]) + gate_attn * _dot(y_attn, wau_ref[...])
        out = x + _dot(merged.astype(BF16), wo_ref[...])
        if final:
            out = _rms_norm(out, fg_ref[...])
        o_ref[sub, :] = out


def _merge(x, g, w_c, b_gate, y_pool, attn, w_pool_up, w_attn_up, w_out, final_g, final):
    B, S, D = x.shape
    rows = MERGE_ROWS
    row_spec = lambda width: pl.BlockSpec((None, rows, width), lambda b, s: (b, s, 0))
    full = lambda shape: pl.BlockSpec(shape, lambda b, s: (0,) * len(shape), pipeline_mode=pl.Buffered(1))
    return pl.pallas_call(
        functools.partial(_merge_kernel, final=final),
        grid=(B, S // rows),
        in_specs=[row_spec(D), full((1, D)), full(w_c.shape), full((1, 2 * D)),
                  row_spec(POOL_WIDTH), row_spec(ATTN_WIDTH),
                  full(w_pool_up.shape), full(w_attn_up.shape), full(w_out.shape), full((1, D))],
        out_specs=row_spec(D),
        out_shape=jax.ShapeDtypeStruct((B, S, D), F32),
        compiler_params=pltpu.CompilerParams(
            dimension_semantics=("arbitrary", "arbitrary"), vmem_limit_bytes=VMEM_LIMIT),
        name="merge_out",
    )(x, g, w_c, b_gate, y_pool, attn, w_pool_up, w_attn_up, w_out, final_g)


def kernel(x, norm_g, w_in, b_gate, pool_w, pool_scale, w_pool_up, w_attn_up, w_out, final_g):
    depth = norm_g.shape[0]
    n_a = 2 * POOL_WIDTH + 3 * ATTN_WIDTH
    q_lo, q_hi = 2 * POOL_WIDTH, 2 * POOL_WIDTH + ATTN_WIDTH
    col_scale = jnp.ones((n_a,), F32).at[q_lo:q_hi].set(HEAD_DIM ** -0.5)
    final_g2 = final_g.reshape(1, D_MODEL)
    for l in range(depth):
        w_a = (w_in[l, :, :n_a] * col_scale).astype(BF16)
        w_c = w_in[l, :, n_a:].astype(BF16)
        g = norm_g[l].reshape(1, D_MODEL)
        y_pool, q, k, v = _proj_pool(x, g, w_a, pool_w[l].astype(BF16), pool_scale[l].reshape(1, POOL_WIDTH))
        attn = _attention(q, k, v)
        x = _merge(x, g, w_c, b_gate[l].reshape(1, 2 * D_MODEL), y_pool, attn,
                   w_pool_up[l].astype(BF16), w_attn_up[l].astype(BF16), w_out[l].astype(BF16),
                   final_g2, final=(l == depth - 1))
    return x
```

```python
import functools

import jax
import jax.numpy as jnp
from jax import lax
from jax.experimental import pallas as pl
from jax.experimental.pallas import tpu as pltpu

D_MODEL = 1024
POOL_WIDTH = 512
POOL_WINDOWS = (2, 4, 8, 16)
POOL_GROUP = 128
N_HEADS = 8
HEAD_DIM = 64
ATTN_WIDTH = N_HEADS * HEAD_DIM
RMS_EPS = 1e-6

LANES = 128
HALO = 16
PROJ_ROWS = 1024
PROJ_SUB = 256
MERGE_ROWS = 1024
MERGE_SUB = 256
ATTN_ROWS = 1024
SUB = 64
WIN = 256
HALF = WIN // 2
GROUP = 16
SKEW = 8
LOG_WEIGHT_FLOOR = -110.0
LOG2_E = 1.4426950408889634
MASKED_LOGIT = -1e30
VMEM_LIMIT = 48 * 1024 * 1024

F32 = jnp.float32
BF16 = jnp.bfloat16


def _rms_norm(x, g):
    ms = jnp.mean(x * x, axis=-1, keepdims=True)
    return x * lax.rsqrt(ms + RMS_EPS) * g


def _silu(z):
    return z * jax.nn.sigmoid(z)


def _dot(a, b):
    return jnp.dot(a, b, preferred_element_type=F32)


def _proj_pool_kernel(x_ref, g_ref, w_ref, pw_ref, ps_ref,
                      yp_ref, q_ref, k_ref, v_ref, ext_ref):
    s = pl.program_id(1)
    rows = x_ref.shape[0]
    pw, aw = POOL_WIDTH, ATTN_WIDTH

    @pl.when(s == 0)
    def _():
        ext_ref[0:HALO, :] = jnp.zeros((HALO, pw), F32)

    for r0 in range(0, rows, PROJ_SUB):
        sub = slice(r0, r0 + PROJ_SUB)
        h = _rms_norm(x_ref[sub, :], g_ref[...]).astype(BF16)
        ext_ref[HALO + r0:HALO + r0 + PROJ_SUB, :] = _dot(h, w_ref[:, 0:pw])
        z = _dot(h, w_ref[:, pw:2 * pw])
        q_ref[sub, :] = _dot(h, w_ref[:, 2 * pw:2 * pw + aw]).astype(BF16)
        k_ref[sub, :] = _dot(h, w_ref[:, 2 * pw + aw:2 * pw + 2 * aw]).astype(BF16)
        v_ref[sub, :] = _dot(h, w_ref[:, 2 * pw + 2 * aw:2 * pw + 3 * aw]).astype(BF16)

        head_pos = s * rows + r0 + lax.broadcasted_iota(jnp.int32, (HALO, 1), 0)
        for g, w in enumerate(POOL_WINDOWS):
            cols = slice(g * POOL_GROUP, (g + 1) * POOL_GROUP)
            e = ext_ref[r0:r0 + HALO + PROJ_SUB, cols]
            acc, span = e, 1
            while span < w:
                acc = acc + pltpu.roll(acc, span, axis=0)
                span *= 2
            ug, acc = e[HALO:], acc[HALO:]
            cnt = jnp.minimum(head_pos + 1, w).astype(F32)
            pooled = jnp.concatenate([acc[:HALO] / cnt, acc[HALO:] * (1.0 / w)], axis=0) - ug
            mixed = _dot(pooled.astype(BF16), pw_ref[g])
            y = mixed * ps_ref[:, cols] * _silu(z[:, cols])
            yp_ref[sub, cols] = y.astype(BF16)
    ext_ref[0:HALO, :] = ext_ref[rows:rows + HALO, :]


def _proj_pool(x, g, w_a, pool_w, pool_scale):
    B, S, D = x.shape
    rows = PROJ_ROWS
    n_cols = w_a.shape[1]
    act = jax.ShapeDtypeStruct((B, S, ATTN_WIDTH), BF16)
    row_spec = lambda width: pl.BlockSpec((None, rows, width), lambda b, s: (b, s, 0))
    full = lambda shape: pl.BlockSpec(shape, lambda b, s: (0,) * len(shape), pipeline_mode=pl.Buffered(1))
    return pl.pallas_call(
        _proj_pool_kernel,
        grid=(B, S // rows),
        in_specs=[row_spec(D), full((1, D)), full((D, n_cols)),
                  full((len(POOL_WINDOWS), POOL_GROUP, POOL_GROUP)), full((1, POOL_WIDTH))],
        out_specs=[row_spec(POOL_WIDTH), row_spec(ATTN_WIDTH), row_spec(ATTN_WIDTH), row_spec(ATTN_WIDTH)],
        out_shape=[jax.ShapeDtypeStruct((B, S, POOL_WIDTH), BF16), act, act, act],
        scratch_shapes=[pltpu.VMEM((rows + HALO, POOL_WIDTH), F32)],
        compiler_params=pltpu.CompilerParams(
            dimension_semantics=("arbitrary", "arbitrary"), vmem_limit_bytes=VMEM_LIMIT),
        name="proj_pool",
    )(x, g, w_a, pool_w, pool_scale)


def _stack_heads(q2):
    lane = lax.broadcasted_iota(jnp.int32, q2.shape, 1)
    zero = jnp.zeros_like(q2)
    return jnp.concatenate([jnp.where(lane < HEAD_DIM, q2, zero),
                            jnp.where(lane >= HEAD_DIM, q2, zero)], axis=0)


def _windows(blocks, sufw):
    n = len(blocks)
    logits, log_betas, sums, out = [None] * n, [None] * n, [None] * n, [None] * n

    def stage_logits(i):
        qs, k2 = blocks[i][0], blocks[i][1]
        logits[i] = lax.dot_general(qs, k2, (((1,), (1,)), ((), ())), preferred_element_type=F32)

    def stage_suffix_sums(i):
        _, _, _, bias_a, bias_b, _ = blocks[i]
        la, lb = logits[i][:, :HALF], logits[i][:, HALF:]
        if bias_a is not None:
            la = la + bias_a
        if bias_b is not None:
            lb = lb + bias_b
        lg = jnp.concatenate([la, lb], axis=1)
        softplus = jnp.log(1.0 + jnp.exp2(jnp.abs(lg) * (-LOG2_E)))
        log_beta = jnp.minimum(lg, 0.0) - softplus
        l1m = log_beta - lg
        l1m_hi = l1m.astype(BF16)
        l1m_lo = (l1m - l1m_hi.astype(F32)).astype(BF16)
        x = jnp.concatenate(
            [jnp.concatenate([l1m_hi[:, :HALF], l1m_lo[:, :HALF]], axis=1),
             jnp.concatenate([l1m_hi[:, HALF:], l1m_lo[:, HALF:]], axis=1)], axis=0)
        logits[i], log_betas[i] = None, log_beta
        sums[i] = _dot(x, sufw)

    def stage_weights(i):
        r, carry = sums[i], blocks[i][5]
        suf_a, tot_a = r[:2 * SUB, :HALF], r[:2 * SUB, HALF:]
        suf_b, tot_b = r[2 * SUB:, :HALF], r[2 * SUB:, HALF:]
        if carry is None:
            later = jnp.concatenate([suf_a + tot_b, suf_b], axis=1)
            new_carry = tot_a + tot_b
        else:
            later = jnp.concatenate([suf_a + (tot_b + carry), suf_b + carry], axis=1)
            new_carry = carry + (tot_a + tot_b)
        p = jnp.exp(log_betas[i] + later).astype(BF16)
        sums[i], log_betas[i] = None, None
        out[i] = (_dot(p, blocks[i][2]), new_carry)

    for i in range(n + 2 * SKEW):
        if i < n:
            stage_logits(i)
        if 0 <= i - SKEW < n:
            stage_suffix_sums(i - SKEW)
        if 0 <= i - 2 * SKEW < n:
            stage_weights(i - 2 * SKEW)
    return out


def _attn_kernel(q_ref, k_ref, v_ref, sufw_ref, o_ref, acc_ref, carry_ref, worst_ref):
    step = pl.program_id(2)
    row_in_sub = lax.broadcasted_iota(jnp.int32, (2 * SUB, HALF), 0) & (SUB - 1)
    col_a = lax.broadcasted_iota(jnp.int32, (2 * SUB, HALF), 1)
    col_b = col_a + HALF
    lane = lax.broadcasted_iota(jnp.int32, (SUB, LANES), 1)

    def masked(key_offset_limit):
        return (jnp.where(col_a < key_offset_limit, 0.0, MASKED_LOGIT),
                jnp.where(col_b < key_offset_limit, 0.0, MASKED_LOGIT))

    _, diag_bias = masked((WIN - SUB) + row_in_sub)

    def trip(it, _):
        base_row = pl.multiple_of(it * (GROUP * SUB), GROUP * SUB)
        base_q = step * ATTN_ROWS + base_row
        at_sequence_start = base_q == 0

        def q_rows(g):
            return _stack_heads(q_ref[pl.ds(pl.multiple_of(base_row + g * SUB, SUB), SUB), :])

        def write_rows(g, acc):
            o_ref[pl.ds(pl.multiple_of(base_row + g * SUB, SUB), SUB), :] = (
                jnp.where(lane < HEAD_DIM, acc[:SUB], acc[SUB:]).astype(BF16))

        def first_windows(blocks):
            worst = None
            for g, (pv, tot) in enumerate(_windows(blocks, sufw_ref[...])):
                acc_ref[g] = pv
                carry_ref[g] = tot
                write_rows(g, pv)
                worst = tot if worst is None else jnp.maximum(worst, tot)
            worst_ref[0] = jnp.max(worst)

        @pl.when(at_sequence_start)
        def _():
            blocks = []
            for g in range(GROUP):
                start = max(g * SUB - (WIN - SUB), 0)
                blocks.append((q_rows(g), k_ref[start:start + WIN, :], v_ref[start:start + WIN, :])
                              + masked(g * SUB - start + row_in_sub) + (None,))
            first_windows(blocks)

        @pl.when(jnp.logical_not(at_sequence_start))
        def _():
            blocks = []
            for g in range(GROUP):
                start = pl.multiple_of(base_q + g * SUB - (WIN - SUB), SUB)
                blocks.append((q_rows(g), k_ref[pl.ds(start, WIN), :], v_ref[pl.ds(start, WIN), :],
                               None, diag_bias, None))
            first_windows(blocks)

        @pl.when(worst_ref[0] > LOG_WEIGHT_FLOOR)
        def _():
            def sweep_block(g, _):
                qs = q_rows(g)

                def cond(c):
                    start, m = c
                    return jnp.logical_and(start > 0, m > LOG_WEIGHT_FLOOR)

                def body(c):
                    start, _ = c
                    nxt = pl.multiple_of(jnp.maximum(start - WIN, 0), SUB)
                    block = (qs, k_ref[pl.ds(nxt, WIN), :], v_ref[pl.ds(nxt, WIN), :]) + masked(start - nxt)
                    (pv, tot), = _windows([block + (carry_ref[g],)], sufw_ref[...])
                    acc_ref[g] = acc_ref[g] + pv
                    carry_ref[g] = tot
                    return nxt, jnp.max(tot)

                start0 = jnp.maximum(base_q + g * SUB - (WIN - SUB), 0)
                lax.while_loop(cond, body, (start0, jnp.max(carry_ref[g])))
                write_rows(g, acc_ref[g])
                return 0

            lax.fori_loop(0, GROUP, sweep_block, 0)

        return 0

    lax.fori_loop(0, ATTN_ROWS // (GROUP * SUB), trip, 0)


def _suffix_sum_weights():
    j = jnp.arange(HALF)[:, None]
    c = jnp.arange(HALF)[None, :]
    block = jnp.concatenate([(j > c), jnp.ones((HALF, HALF), bool)], axis=1)
    return jnp.concatenate([block, block], axis=0).astype(BF16)


def _attention(q, k, v):
    B, S, _ = q.shape
    n_pairs = ATTN_WIDTH // LANES
    q_spec = pl.BlockSpec((None, ATTN_ROWS, LANES), lambda b, p, s: (b, s, p))
    kv_spec = pl.BlockSpec((None, S, LANES), lambda b, p, s: (b, 0, p))
    return pl.pallas_call(
        _attn_kernel,
        grid=(B, n_pairs, S // ATTN_ROWS),
        in_specs=[q_spec, kv_spec, kv_spec, pl.BlockSpec((WIN, WIN), lambda b, p, s: (0, 0))],
        out_specs=q_spec,
        out_shape=jax.ShapeDtypeStruct((B, S, ATTN_WIDTH), BF16),
        scratch_shapes=[pltpu.VMEM((GROUP, 2 * SUB, LANES), F32), pltpu.VMEM((GROUP, 2 * SUB, LANES), F32),
                        pltpu.SMEM((1,), F32)],
        compiler_params=pltpu.CompilerParams(
            dimension_semantics=("arbitrary", "arbitrary", "arbitrary"), vmem_limit_bytes=VMEM_LIMIT),
        name="stickbreak_attn",
    )(q, k, v, _suffix_sum_weights())


def _merge_kernel(x_ref, g_ref, w_ref, b_ref, yp_ref, at_ref, wpu_ref, wau_ref, wo_ref, fg_ref,
                  o_ref, *, final):
    aw, d = ATTN_WIDTH, D_MODEL
    for r0 in range(0, x_ref.shape[0], MERGE_SUB):
        sub = slice(r0, r0 + MERGE_SUB)
        x = x_ref[sub, :]
        h = _rms_norm(x, g_ref[...]).astype(BF16)
        z_attn = _dot(h, w_ref[:, 0:aw])
        y_attn = (at_ref[sub, :].astype(F32) * _silu(z_attn)).astype(BF16)
        gate_pool = jax.nn.sigmoid(_dot(h, w_ref[:, aw:aw + d]) + b_ref[:, 0:d])
        gate_attn = jax.nn.sigmoid(_dot(h, w_ref[:, aw + d:aw + 2 * d]) + b_ref[:, d:2 * d])
        merged = gate_pool * _dot(yp_ref[sub, :], wpu_ref[...]) + gate_attn * _dot(y_attn, wau_ref[...])
        out = x + _dot(merged.astype(BF16), wo_ref[...])
        if final:
            out = _rms_norm(out, fg_ref[...])
        o_ref[sub, :] = out


def _merge(x, g, w_c, b_gate, y_pool, attn, w_pool_up, w_attn_up, w_out, final_g, final):
    B, S, D = x.shape
    rows = MERGE_ROWS
    row_spec = lambda width: pl.BlockSpec((None, rows, width), lambda b, s: (b, s, 0))
    full = lambda shape: pl.BlockSpec(shape, lambda b, s: (0,) * len(shape), pipeline_mode=pl.Buffered(1))
    return pl.pallas_call(
        functools.partial(_merge_kernel, final=final),
        grid=(B, S // rows),
        in_specs=[row_spec(D), full((1, D)), full(w_c.shape), full((1, 2 * D)),
                  row_spec(POOL_WIDTH), row_spec(ATTN_WIDTH),
                  full(w_pool_up.shape), full(w_attn_up.shape), full(w_out.shape), full((1, D))],
        out_specs=row_spec(D),
        out_shape=jax.ShapeDtypeStruct((B, S, D), F32),
        compiler_params=pltpu.CompilerParams(
            dimension_semantics=("arbitrary", "arbitrary"), vmem_limit_bytes=VMEM_LIMIT),
        name="merge_out",
    )(x, g, w_c, b_gate, y_pool, attn, w_pool_up, w_attn_up, w_out, final_g)


def kernel(x, norm_g, w_in, b_gate, pool_w, pool_scale, w_pool_up, w_attn_up, w_out, final_g):
    depth = norm_g.shape[0]
    n_a = 2 * POOL_WIDTH + 3 * ATTN_WIDTH
    q_lo, q_hi = 2 * POOL_WIDTH, 2 * POOL_WIDTH + ATTN_WIDTH
    col_scale = jnp.ones((n_a,), F32).at[q_lo:q_hi].set(HEAD_DIM ** -0.5)
    final_g2 = final_g.reshape(1, D_MODEL)
    for l in range(depth):
        w_a = (w_in[l, :, :n_a] * col_scale).astype(BF16)
        w_c = w_in[l, :, n_a:].astype(BF16)
        g = norm_g[l].reshape(1, D_MODEL)
        y_pool, q, k, v = _proj_pool(x, g, w_a, pool_w[l].astype(BF16), pool_scale[l].reshape(1, POOL_WIDTH))
        attn = _attention(q, k, v)
        x = _merge(x, g, w_c, b_gate[l].reshape(1, 2 * D_MODEL), y_pool, attn,
                   w_pool_up[l].astype(BF16), w_attn_up[l].astype(BF16), w_out[l].astype(BF16),
                   final_g2, final=(l == depth - 1))
    return x
```

```python
import functools

import jax
import jax.numpy as jnp
from jax import lax
from jax.experimental import pallas as pl
from jax.experimental.pallas import tpu as pltpu

D_MODEL = 1024
POOL_WIDTH = 512
POOL_WINDOWS = (2, 4, 8, 16)
POOL_GROUP = 128
N_HEADS = 8
HEAD_DIM = 64
ATTN_WIDTH = N_HEADS * HEAD_DIM
RMS_EPS = 1e-6

LANES = 128
HALO = 16
PROJ_ROWS = 1024
PROJ_SUB = 256
MERGE_ROWS = 1024
MERGE_SUB = 256
ATTN_ROWS = 1024
SUB = 64
WIN = 256
HALF = WIN // 2
GROUP = 16
SKEW = 8
LOG_WEIGHT_FLOOR = -110.0
LOG2_E = 1.4426950408889634
MASKED_LOGIT = -1e30
VMEM_LIMIT = 48 * 1024 * 1024

F32 = jnp.float32
BF16 = jnp.bfloat16


def _rms_norm(x, g):
    ms = jnp.mean(x * x, axis=-1, keepdims=True)
    return x * lax.rsqrt(ms + RMS_EPS) * g


def _silu(z):
    return z * jax.nn.sigmoid(z)


def _dot(a, b):
    return jnp.dot(a, b, preferred_element_type=F32)


def _proj_pool_kernel(x_ref, g_ref, w_ref, pw_ref, ps_ref,
                      yp_ref, q_ref, k_ref, v_ref, ext_ref):
    s = pl.program_id(1)
    rows = x_ref.shape[0]
    pw, aw = POOL_WIDTH, ATTN_WIDTH

    @pl.when(s == 0)
    def _():
        ext_ref[0:HALO, :] = jnp.zeros((HALO, pw), F32)

    for r0 in range(0, rows, PROJ_SUB):
        sub = slice(r0, r0 + PROJ_SUB)
        h = _rms_norm(x_ref[sub, :], g_ref[...]).astype(BF16)
        ext_ref[HALO + r0:HALO + r0 + PROJ_SUB, :] = _dot(h, w_ref[:, 0:pw])
        z = _dot(h, w_ref[:, pw:2 * pw])
        q_ref[sub, :] = _dot(h, w_ref[:, 2 * pw:2 * pw + aw]).astype(BF16)
        k_ref[sub, :] = _dot(h, w_ref[:, 2 * pw + aw:2 * pw + 2 * aw]).astype(BF16)
        v_ref[sub, :] = _dot(h, w_ref[:, 2 * pw + 2 * aw:2 * pw + 3 * aw]).astype(BF16)

        head_pos = s * rows + r0 + lax.broadcasted_iota(jnp.int32, (HALO, 1), 0)
        for g, w in enumerate(POOL_WINDOWS):
            cols = slice(g * POOL_GROUP, (g + 1) * POOL_GROUP)
            e = ext_ref[r0:r0 + HALO + PROJ_SUB, cols]
            acc, span = e, 1
            while span < w:
                acc = acc + pltpu.roll(acc, span, axis=0)
                span *= 2
            ug, acc = e[HALO:], acc[HALO:]
            cnt = jnp.minimum(head_pos + 1, w).astype(F32)
            pooled = jnp.concatenate([acc[:HALO] / cnt, acc[HALO:] * (1.0 / w)], axis=0) - ug
            mixed = _dot(pooled.astype(BF16), pw_ref[g])
            y = mixed * ps_ref[:, cols] * _silu(z[:, cols])
            yp_ref[sub, cols] = y.astype(BF16)
    ext_ref[0:HALO, :] = ext_ref[rows:rows + HALO, :]


def _proj_pool(x, g, w_a, pool_w, pool_scale):
    B, S, D = x.shape
    rows = PROJ_ROWS
    n_cols = w_a.shape[1]
    act = jax.ShapeDtypeStruct((B, S, ATTN_WIDTH), BF16)
    row_spec = lambda width: pl.BlockSpec((None, rows, width), lambda b, s: (b, s, 0))
    full = lambda shape: pl.BlockSpec(shape, lambda b, s: (0,) * len(shape), pipeline_mode=pl.Buffered(1))
    return pl.pallas_call(
        _proj_pool_kernel,
        grid=(B, S // rows),
        in_specs=[row_spec(D), full((1, D)), full((D, n_cols)),
                  full((len(POOL_WINDOWS), POOL_GROUP, POOL_GROUP)), full((1, POOL_WIDTH))],
        out_specs=[row_spec(POOL_WIDTH), row_spec(ATTN_WIDTH), row_spec(ATTN_WIDTH), row_spec(ATTN_WIDTH)],
        out_shape=[jax.ShapeDtypeStruct((B, S, POOL_WIDTH), BF16), act, act, act],
        scratch_shapes=[pltpu.VMEM((rows + HALO, POOL_WIDTH), F32)],
        compiler_params=pltpu.CompilerParams(
            dimension_semantics=("arbitrary", "arbitrary"), vmem_limit_bytes=VMEM_LIMIT),
        name="proj_pool",
    )(x, g, w_a, pool_w, pool_scale)


def _stack_heads(q2):
    lane = lax.broadcasted_iota(jnp.int32, q2.shape, 1)
    zero = jnp.zeros_like(q2)
    return jnp.concatenate([jnp.where(lane < HEAD_DIM, q2, zero),
                            jnp.where(lane >= HEAD_DIM, q2, zero)], axis=0)


def _windows(blocks, sufw):
    n = len(blocks)
    logits, log_betas, sums, out = [None] * n, [None] * n, [None] * n, [None] * n

    def stage_logits(i):
        qs, k2 = blocks[i][0], blocks[i][1]
        logits[i] = lax.dot_general(qs, k2, (((1,), (1,)), ((), ())), preferred_element_type=F32)

    def stage_suffix_sums(i):
        _, _, _, bias_a, bias_b, _ = blocks[i]
        la, lb = logits[i][:, :HALF], logits[i][:, HALF:]
        if bias_a is not None:
            la = la + bias_a
        if bias_b is not None:
            lb = lb + bias_b
        lg = jnp.concatenate([la, lb], axis=1)
        softplus = jnp.log(1.0 + jnp.exp2(jnp.abs(lg) * (-LOG2_E)))
        log_beta = jnp.minimum(lg, 0.0) - softplus
        l1m = log_beta - lg
        logits[i], log_betas[i] = None, log_beta
        sums[i] = (_dot(l1m.astype(BF16), sufw), jnp.sum(l1m, axis=1, keepdims=True))

    def stage_weights(i):
        (later, total), carry = sums[i], blocks[i][5]
        if carry is None:
            new_carry = total
        else:
            later = later + carry
            new_carry = carry + total
        p = jnp.exp(log_betas[i] + later).astype(BF16)
        sums[i], log_betas[i] = None, None
        out[i] = (_dot(p, blocks[i][2]), new_carry)

    for i in range(n + 2 * SKEW):
        if i < n:
            stage_logits(i)
        if 0 <= i - SKEW < n:
            stage_suffix_sums(i - SKEW)
        if 0 <= i - 2 * SKEW < n:
            stage_weights(i - 2 * SKEW)
    return out


def _attn_kernel(q_ref, k_ref, v_ref, sufw_ref, o_ref, acc_ref, carry_ref, worst_ref):
    step = pl.program_id(2)
    row_in_sub = lax.broadcasted_iota(jnp.int32, (2 * SUB, HALF), 0) & (SUB - 1)
    col_a = lax.broadcasted_iota(jnp.int32, (2 * SUB, HALF), 1)
    col_b = col_a + HALF
    lane = lax.broadcasted_iota(jnp.int32, (SUB, LANES), 1)

    def masked(key_offset_limit):
        return (jnp.where(col_a < key_offset_limit, 0.0, MASKED_LOGIT),
                jnp.where(col_b < key_offset_limit, 0.0, MASKED_LOGIT))

    _, diag_bias = masked((WIN - SUB) + row_in_sub)

    def trip(it, _):
        base_row = pl.multiple_of(it * (GROUP * SUB), GROUP * SUB)
        base_q = step * ATTN_ROWS + base_row
        at_sequence_start = base_q == 0

        def q_rows(g):
            return _stack_heads(q_ref[pl.ds(pl.multiple_of(base_row + g * SUB, SUB), SUB), :])

        def write_rows(g, acc):
            o_ref[pl.ds(pl.multiple_of(base_row + g * SUB, SUB), SUB), :] = (
                jnp.where(lane < HEAD_DIM, acc[:SUB], acc[SUB:]).astype(BF16))

        def first_windows(blocks):
            worst = None
            for g, (pv, tot) in enumerate(_windows(blocks, sufw_ref[...])):
                acc_ref[g] = pv
                carry_ref[g] = tot
                write_rows(g, pv)
                worst = tot if worst is None else jnp.maximum(worst, tot)
            worst_ref[0] = jnp.max(worst)

        @pl.when(at_sequence_start)
        def _():
            blocks = []
            for g in range(GROUP):
                start = max(g * SUB - (WIN - SUB), 0)
                blocks.append((q_rows(g), k_ref[start:start + WIN, :], v_ref[start:start + WIN, :])
                              + masked(g * SUB - start + row_in_sub) + (None,))
            first_windows(blocks)

        @pl.when(jnp.logical_not(at_sequence_start))
        def _():
            blocks = []
            for g in range(GROUP):
                start = pl.multiple_of(base_q + g * SUB - (WIN - SUB), SUB)
                blocks.append((q_rows(g), k_ref[pl.ds(start, WIN), :], v_ref[pl.ds(start, WIN), :],
                               None, diag_bias, None))
            first_windows(blocks)

        @pl.when(worst_ref[0] > LOG_WEIGHT_FLOOR)
        def _():
            def sweep_block(g, _):
                qs = q_rows(g)

                def cond(c):
                    start, m = c
                    return jnp.logical_and(start > 0, m > LOG_WEIGHT_FLOOR)

                def body(c):
                    start, _ = c
                    nxt = pl.multiple_of(jnp.maximum(start - WIN, 0), SUB)
                    block = (qs, k_ref[pl.ds(nxt, WIN), :], v_ref[pl.ds(nxt, WIN), :]) + masked(start - nxt)
                    (pv, tot), = _windows([block + (carry_ref[g],)], sufw_ref[...])
                    acc_ref[g] = acc_ref[g] + pv
                    carry_ref[g] = tot
                    return nxt, jnp.max(tot)

                start0 = jnp.maximum(base_q + g * SUB - (WIN - SUB), 0)
                lax.while_loop(cond, body, (start0, jnp.max(carry_ref[g])))
                write_rows(g, acc_ref[g])
                return 0

            lax.fori_loop(0, GROUP, sweep_block, 0)

        return 0

    lax.fori_loop(0, ATTN_ROWS // (GROUP * SUB), trip, 0)


def _suffix_sum_weights():
    j = jnp.arange(WIN)[:, None]
    c = jnp.arange(WIN)[None, :]
    return (j > c).astype(BF16)


def _attention(q, k, v):
    B, S, _ = q.shape
    n_pairs = ATTN_WIDTH // LANES
    q_spec = pl.BlockSpec((None, ATTN_ROWS, LANES), lambda b, p, s: (b, s, p))
    kv_spec = pl.BlockSpec((None, S, LANES), lambda b, p, s: (b, 0, p))
    return pl.pallas_call(
        _attn_kernel,
        grid=(B, n_pairs, S // ATTN_ROWS),
        in_specs=[q_spec, kv_spec, kv_spec, pl.BlockSpec((WIN, WIN), lambda b, p, s: (0, 0))],
        out_specs=q_spec,
        out_shape=jax.ShapeDtypeStruct((B, S, ATTN_WIDTH), BF16),
        scratch_shapes=[pltpu.VMEM((GROUP, 2 * SUB, LANES), F32), pltpu.VMEM((GROUP, 2 * SUB, 1), F32),
                        pltpu.SMEM((1,), F32)],
        compiler_params=pltpu.CompilerParams(
            dimension_semantics=("arbitrary", "arbitrary", "arbitrary"), vmem_limit_bytes=VMEM_LIMIT),
        name="stickbreak_attn",
    )(q, k, v, _suffix_sum_weights())


def _merge_kernel(x_ref, g_ref, w_ref, b_ref, yp_ref, at_ref, wpu_ref, wau_ref, wo_ref, fg_ref,
                  o_ref, *, final):
    aw, d = ATTN_WIDTH, D_MODEL
    for r0 in range(0, x_ref.shape[0], MERGE_SUB):
        sub = slice(r0, r0 + MERGE_SUB)
        x = x_ref[sub, :]
        h = _rms_norm(x, g_ref[...]).astype(BF16)
        z_attn = _dot(h, w_ref[:, 0:aw])
        y_attn = (at_ref[sub, :].astype(F32) * _silu(z_attn)).astype(BF16)
        gate_pool = jax.nn.sigmoid(_dot(h, w_ref[:, aw:aw + d]) + b_ref[:, 0:d])
        gate_attn = jax.nn.sigmoid(_dot(h, w_ref[:, aw + d:aw + 2 * d]) + b_ref[:, d:2 * d])
        merged = gate_pool * _dot(yp_ref[sub, :], wpu_ref[...]) + gate_attn * _dot(y_attn, wau_ref[...])
        out = x + _dot(merged.astype(BF16), wo_ref[...])
        if final:
            out = _rms_norm(out, fg_ref[...])
        o_ref[sub, :] = out


def _merge(x, g, w_c, b_gate, y_pool, attn, w_pool_up, w_attn_up, w_out, final_g, final):
    B, S, D = x.shape
    rows = MERGE_ROWS
    row_spec = lambda width: pl.BlockSpec((None, rows, width), lambda b, s: (b, s, 0))
    full = lambda shape: pl.BlockSpec(shape, lambda b, s: (0,) * len(shape), pipeline_mode=pl.Buffered(1))
    return pl.pallas_call(
        functools.partial(_merge_kernel, final=final),
        grid=(B, S // rows),
        in_specs=[row_spec(D), full((1, D)), full(w_c.shape), full((1, 2 * D)),
                  row_spec(POOL_WIDTH), row_spec(ATTN_WIDTH),
                  full(w_pool_up.shape), full(w_attn_up.shape), full(w_out.shape), full((1, D))],
        out_specs=row_spec(D),
        out_shape=jax.ShapeDtypeStruct((B, S, D), F32),
        compiler_params=pltpu.CompilerParams(
            dimension_semantics=("arbitrary", "arbitrary"), vmem_limit_bytes=VMEM_LIMIT),
        name="merge_out",
    )(x, g, w_c, b_gate, y_pool, attn, w_pool_up, w_attn_up, w_out, final_g)


def kernel(x, norm_g, w_in, b_gate, pool_w, pool_scale, w_pool_up, w_attn_up, w_out, final_g):
    depth = norm_g.shape[0]
    n_a = 2 * POOL_WIDTH + 3 * ATTN_WIDTH
    q_lo, q_hi = 2 * POOL_WIDTH, 2 * POOL_WIDTH + ATTN_WIDTH
    col_scale = jnp.ones((n_a,), F32).at[q_lo:q_hi].set(HEAD_DIM ** -0.5)
    final_g2 = final_g.reshape(1, D_MODEL)
    for l in range(depth):
        w_a = (w_in[l, :, :n_a] * col_scale).astype(BF16)
        w_c = w_in[l, :, n_a:].astype(BF16)
        g = norm_g[l].reshape(1, D_MODEL)
        y_pool, q, k, v = _proj_pool(x, g, w_a, pool_w[l].astype(BF16), pool_scale[l].reshape(1, POOL_WIDTH))
        attn = _attention(q, k, v)
        x = _merge(x, g, w_c, b_gate[l].reshape(1, 2 * D_MODEL), y_pool, attn,
                   w_pool_up[l].astype(BF16), w_attn_up[l].astype(BF16), w_out[l].astype(BF16),
                   final_g2, final=(l == depth - 1))
    return x
```

```python
import functools

import jax
import jax.numpy as jnp
from jax import lax
from jax.experimental import pallas as pl
from jax.experimental.pallas import tpu as pltpu

D_MODEL = 1024
POOL_WIDTH = 512
POOL_WINDOWS = (2, 4, 8, 16)
POOL_GROUP = 128
N_HEADS = 8
HEAD_DIM = 64
ATTN_WIDTH = N_HEADS * HEAD_DIM
RMS_EPS = 1e-6

LANES = 128
N_PAIRS = ATTN_WIDTH // LANES
HALO = 16
FUSE_ROWS = 1024
FUSE_SUB = 256
MERGE_ROWS = 1024
MERGE_SUB = 256
SUB = 64
WIN = 256
HALF = WIN // 2
SKEW = 8
LOG_WEIGHT_FLOOR = -110.0
LOG2_E = 1.4426950408889634
MASKED_LOGIT = -1e30
VMEM_LIMIT = 52 * 1024 * 1024

F32 = jnp.float32
BF16 = jnp.bfloat16


def _rms_norm(x, g):
    ms = jnp.mean(x * x, axis=-1, keepdims=True)
    return x * lax.rsqrt(ms + RMS_EPS) * g


def _silu(z):
    return z * jax.nn.sigmoid(z)


def _dot(a, b):
    return jnp.dot(a, b, preferred_element_type=F32)


def _stack_heads(q2):
    lane = lax.broadcasted_iota(jnp.int32, q2.shape, 1)
    zero = jnp.zeros_like(q2)
    return jnp.concatenate([jnp.where(lane < HEAD_DIM, q2, zero),
                            jnp.where(lane >= HEAD_DIM, q2, zero)], axis=0)


def _unstack_heads(acc):
    lane = lax.broadcasted_iota(jnp.int32, (SUB, LANES), 1)
    return jnp.where(lane < HEAD_DIM, acc[:SUB], acc[SUB:])


def _windows(blocks, sufw, fillers=()):
    n = len(blocks)
    logits, log_betas, sums, out = [None] * n, [None] * n, [None] * n, [None] * n

    def stage_logits(i):
        qs_fn, k_fn = blocks[i][0], blocks[i][1]
        logits[i] = lax.dot_general(qs_fn(), k_fn(), (((1,), (1,)), ((), ())),
                                    preferred_element_type=F32)

    def stage_suffix_sums(i):
        _, _, _, bias_a, bias_b, _ = blocks[i]
        la, lb = logits[i][:, :HALF], logits[i][:, HALF:]
        if bias_a is not None:
            la = la + bias_a
        if bias_b is not None:
            lb = lb + bias_b
        lg = jnp.concatenate([la, lb], axis=1)
        softplus = jnp.log(1.0 + jnp.exp2(jnp.abs(lg) * (-LOG2_E)))
        log_beta = jnp.minimum(lg, 0.0) - softplus
        l1m = log_beta - lg
        logits[i], log_betas[i] = None, log_beta
        sums[i] = (_dot(l1m.astype(BF16), sufw), jnp.sum(l1m, axis=1, keepdims=True))

    def stage_weights(i):
        (later, total), carry = sums[i], blocks[i][5]
        if carry is None:
            new_carry = total
        else:
            later = later + carry
            new_carry = carry + total
        p = jnp.exp(log_betas[i] + later).astype(BF16)
        sums[i], log_betas[i] = None, None
        out[i] = (_dot(p, blocks[i][2]()), new_carry)

    n_iter = n + 2 * SKEW
    fillers = list(fillers)
    issued = 0
    for i in range(n_iter):
        if i < n:
            stage_logits(i)
        if 0 <= i - SKEW < n:
            stage_suffix_sums(i - SKEW)
        if 0 <= i - 2 * SKEW < n:
            stage_weights(i - 2 * SKEW)
        while issued < len(fillers) and (issued + 1) * n_iter <= (i + 1) * (len(fillers) + 1):
            fillers[issued]()
            issued += 1
    for f in fillers[issued:]:
        f()
    return out


def _suffix_sum_weights():
    j = jnp.arange(WIN)[:, None]
    c = jnp.arange(WIN)[None, :]
    return (j > c).astype(BF16)


def _proj_attn_kernel(x_ref, g_ref, w_ref, pw_ref, ps_ref, sufw_ref, yp_ref, at_ref,
                      ext_ref, q_s, k_s, v_s, acc_ref, carry_ref, worst_ref):
    s = pl.program_id(1)
    rows = x_ref.shape[0]
    seq0 = s * rows
    pw, aw = POOL_WIDTH, ATTN_WIDTH

    row_in_sub = lax.broadcasted_iota(jnp.int32, (2 * SUB, HALF), 0) & (SUB - 1)
    col_a = lax.broadcasted_iota(jnp.int32, (2 * SUB, HALF), 1)
    col_b = col_a + HALF

    def masked(key_offset_limit):
        return (jnp.where(col_a < key_offset_limit, 0.0, MASKED_LOGIT),
                jnp.where(col_b < key_offset_limit, 0.0, MASKED_LOGIT))

    _, diag_bias = masked((WIN - SUB) + row_in_sub)

    @pl.when(s == 0)
    def _():
        ext_ref[0:HALO, :] = jnp.zeros((HALO, pw), F32)

    def projection_stages(r0):
        sub = slice(r0, r0 + FUSE_SUB)
        seq_rows = pl.ds(pl.multiple_of(seq0 + r0, FUSE_SUB), FUSE_SUB)
        st = {}

        def norm():
            st["h"] = _rms_norm(x_ref[sub, :], g_ref[...]).astype(BF16)

        def pool_values():
            ext_ref[HALO + r0:HALO + r0 + FUSE_SUB, :] = _dot(st["h"], w_ref[:, 0:pw])

        def pool_gate():
            st["z"] = _dot(st["h"], w_ref[:, pw:2 * pw])

        def queries():
            q_s[sub, :] = _dot(st["h"], w_ref[:, 2 * pw:2 * pw + aw]).astype(BF16)

        def keys():
            k_s[seq_rows, :] = _dot(st["h"], w_ref[:, 2 * pw + aw:2 * pw + 2 * aw]).astype(BF16)

        def values():
            v_s[seq_rows, :] = _dot(st["h"], w_ref[:, 2 * pw + 2 * aw:2 * pw + 3 * aw]).astype(BF16)

        def pooling():
            head_pos = seq0 + r0 + lax.broadcasted_iota(jnp.int32, (HALO, 1), 0)
            for g, w in enumerate(POOL_WINDOWS):
                cols = slice(g * POOL_GROUP, (g + 1) * POOL_GROUP)
                e = ext_ref[r0:r0 + HALO + FUSE_SUB, cols]
                acc, span = e, 1
                while span < w:
                    acc = acc + pltpu.roll(acc, span, axis=0)
                    span *= 2
                ug, acc = e[HALO:], acc[HALO:]
                cnt = jnp.minimum(head_pos + 1, w).astype(F32)
                pooled = jnp.concatenate([acc[:HALO] / cnt, acc[HALO:] * (1.0 / w)], axis=0) - ug
                mixed = _dot(pooled.astype(BF16), pw_ref[g])
                y = mixed * ps_ref[:, cols] * _silu(st["z"][:, cols])
                yp_ref[sub, cols] = y.astype(BF16)

        return [norm, pool_values, pool_gate, queries, keys, values, pooling]

    def first_window_blocks(r0):
        blocks, slots = [], []
        for row in range(r0, r0 + FUSE_SUB, SUB):
            q0 = seq0 + row
            if row < WIN - SUB:
                start = jnp.maximum(q0 - (WIN - SUB), 0)
                bias_a, bias_b = masked((q0 - start) + row_in_sub)
            else:
                start = q0 - (WIN - SUB)
                bias_a, bias_b = None, diag_bias
            keys = pl.ds(pl.multiple_of(start, SUB), WIN)
            for p in range(N_PAIRS):
                lanes = slice(p * LANES, (p + 1) * LANES)
                blocks.append((functools.partial(lambda r, l: _stack_heads(q_s[r:r + SUB, l]), row, lanes),
                               functools.partial(lambda k, l: k_s[k, l], keys, lanes),
                               functools.partial(lambda k, l: v_s[k, l], keys, lanes),
                               bias_a, bias_b, None))
                slots.append((slice(row, row + SUB), lanes))
        return blocks, slots

    for stage in projection_stages(0):
        stage()
    worst = None
    for r0 in range(0, rows, FUSE_SUB):
        fillers = projection_stages(r0 + FUSE_SUB) if r0 + FUSE_SUB < rows else ()
        blocks, slots = first_window_blocks(r0)
        for (pv, total), (out_rows, lanes) in zip(_windows(blocks, sufw_ref[...], fillers), slots):
            at_ref[out_rows, lanes] = _unstack_heads(pv).astype(BF16)
            worst = total if worst is None else jnp.maximum(worst, total)
    worst_ref[0] = jnp.max(worst)
    ext_ref[0:HALO, :] = ext_ref[rows:rows + HALO, :]

    @pl.when(worst_ref[0] > LOG_WEIGHT_FLOOR)
    def _():
        def sweep_rows(blk, _):
            row = pl.multiple_of(blk * SUB, SUB)
            q0 = seq0 + row
            for p in range(N_PAIRS):
                lanes = slice(p * LANES, (p + 1) * LANES)
                qs_fn = lambda: _stack_heads(q_s[pl.ds(row, SUB), lanes])

                def window(start, limit, carry):
                    keys = pl.ds(pl.multiple_of(start, SUB), WIN)
                    block = (qs_fn, lambda: k_s[keys, lanes], lambda: v_s[keys, lanes]) + masked(limit) + (carry,)
                    (pv, total), = _windows([block], sufw_ref[...])
                    return pv, total

                start0 = jnp.maximum(q0 - (WIN - SUB), 0)
                pv, total = window(start0, (q0 - start0) + row_in_sub, None)
                acc_ref[...] = pv
                carry_ref[...] = total

                def cond(c):
                    start, m = c
                    return jnp.logical_and(start > 0, m > LOG_WEIGHT_FLOOR)

                def body(c):
                    start, _ = c
                    nxt = jnp.maximum(start - WIN, 0)
                    pv, total = window(nxt, start - nxt, carry_ref[...])
                    acc_ref[...] = acc_ref[...] + pv
                    carry_ref[...] = total
                    return nxt, jnp.max(total)

                lax.while_loop(cond, body, (start0, jnp.max(total)))
                at_ref[pl.ds(row, SUB), lanes] = _unstack_heads(acc_ref[...]).astype(BF16)
            return 0

        lax.fori_loop(0, rows // SUB, sweep_rows, 0)


def _proj_attn(x, g, w_a, pool_w, pool_scale):
    B, S, D = x.shape
    rows = FUSE_ROWS
    n_cols = w_a.shape[1]
    row_spec = lambda width: pl.BlockSpec((None, rows, width), lambda b, s: (b, s, 0))
    full = lambda shape: pl.BlockSpec(shape, lambda b, s: (0,) * len(shape), pipeline_mode=pl.Buffered(1))
    return pl.pallas_call(
        _proj_attn_kernel,
        grid=(B, S // rows),
        in_specs=[row_spec(D), full((1, D)), full((D, n_cols)),
                  full((len(POOL_WINDOWS), POOL_GROUP, POOL_GROUP)), full((1, POOL_WIDTH)), full((WIN, WIN))],
        out_specs=[row_spec(POOL_WIDTH), row_spec(ATTN_WIDTH)],
        out_shape=[jax.ShapeDtypeStruct((B, S, POOL_WIDTH), BF16), jax.ShapeDtypeStruct((B, S, ATTN_WIDTH), BF16)],
        scratch_shapes=[pltpu.VMEM((rows + HALO, POOL_WIDTH), F32),
                        pltpu.VMEM((rows, ATTN_WIDTH), BF16),
                        pltpu.VMEM((S, ATTN_WIDTH), BF16),
                        pltpu.VMEM((S, ATTN_WIDTH), BF16),
                        pltpu.VMEM((2 * SUB, LANES), F32),
                        pltpu.VMEM((2 * SUB, 1), F32),
                        pltpu.SMEM((1,), F32)],
        compiler_params=pltpu.CompilerParams(
            dimension_semantics=("arbitrary", "arbitrary"), vmem_limit_bytes=VMEM_LIMIT),
        name="proj_attn",
    )(x, g, w_a, pool_w, pool_scale, _suffix_sum_weights())


def _merge_kernel(x_ref, g_ref, w_ref, b_ref, yp_ref, at_ref, wpu_ref, wau_ref, wo_ref, fg_ref,
                  o_ref, *, final):
    aw, d = ATTN_WIDTH, D_MODEL
    for r0 in range(0, x_ref.shape[0], MERGE_SUB):
        sub = slice(r0, r0 + MERGE_SUB)
        x = x_ref[sub, :]
        h = _rms_norm(x, g_ref[...]).astype(BF16)
        z_attn = _dot(h, w_ref[:, 0:aw])
        y_attn = (at_ref[sub, :].astype(F32) * _silu(z_attn)).astype(BF16)
        gate_pool = jax.nn.sigmoid(_dot(h, w_ref[:, aw:aw + d]) + b_ref[:, 0:d])
        gate_attn = jax.nn.sigmoid(_dot(h, w_ref[:, aw + d:aw + 2 * d]) + b_ref[:, d:2 * d])
        merged = gate_pool * _dot(yp_ref[sub, :], wpu_ref[...]) + gate_attn * _dot(y_attn, wau_ref[...])
        out = x + _dot(merged.astype(BF16), wo_ref[...])
        if final:
            out = _rms_norm(out, fg_ref[...])
        o_ref[sub, :] = out


def _merge(x, g, w_c, b_gate, y_pool, attn, w_pool_up, w_attn_up, w_out, final_g, final):
    B, S, D = x.shape
    rows = MERGE_ROWS
    row_spec = lambda width: pl.BlockSpec((None, rows, width), lambda b, s: (b, s, 0))
    full = lambda shape: pl.BlockSpec(shape, lambda b, s: (0,) * len(shape), pipeline_mode=pl.Buffered(1))
    return pl.pallas_call(
        functools.partial(_merge_kernel, final=final),
        grid=(B, S // rows),
        in_specs=[row_spec(D), full((1, D)), full(w_c.shape), full((1, 2 * D)),
                  row_spec(POOL_WIDTH), row_spec(ATTN_WIDTH),
                  full(w_pool_up.shape), full(w_attn_up.shape), full(w_out.shape), full((1, D))],
        out_specs=row_spec(D),
        out_shape=jax.ShapeDtypeStruct((B, S, D), F32),
        compiler_params=pltpu.CompilerParams(
            dimension_semantics=("arbitrary", "arbitrary"), vmem_limit_bytes=VMEM_LIMIT),
        name="merge_out",
    )(x, g, w_c, b_gate, y_pool, attn, w_pool_up, w_attn_up, w_out, final_g)


def kernel(x, norm_g, w_in, b_gate, pool_w, pool_scale, w_pool_up, w_attn_up, w_out, final_g):
    depth = norm_g.shape[0]
    n_a = 2 * POOL_WIDTH + 3 * ATTN_WIDTH
    q_lo, q_hi = 2 * POOL_WIDTH, 2 * POOL_WIDTH + ATTN_WIDTH
    col_scale = jnp.ones((n_a,), F32).at[q_lo:q_hi].set(HEAD_DIM ** -0.5)
    final_g2 = final_g.reshape(1, D_MODEL)
    for l in range(depth):
        w_a = (w_in[l, :, :n_a] * col_scale).astype(BF16)
        w_c = w_in[l, :, n_a:].astype(BF16)
        g = norm_g[l].reshape(1, D_MODEL)
        y_pool, attn = _proj_attn(x, g, w_a, pool_w[l].astype(BF16), pool_scale[l].reshape(1, POOL_WIDTH))
        x = _merge(x, g, w_c, b_gate[l].reshape(1, 2 * D_MODEL), y_pool, attn,
                   w_pool_up[l].astype(BF16), w_attn_up[l].astype(BF16), w_out[l].astype(BF16),
                   final_g2, final=(l == depth - 1))
    return x
```

```python
import functools

import jax
import jax.numpy as jnp
from jax import lax
from jax.experimental import pallas as pl
from jax.experimental.pallas import tpu as pltpu

D_MODEL = 1024
POOL_WIDTH = 512
POOL_WINDOWS = (2, 4, 8, 16)
POOL_GROUP = 128
N_HEADS = 8
HEAD_DIM = 64
ATTN_WIDTH = N_HEADS * HEAD_DIM
PRE_ATTN_COLS = 2 * POOL_WIDTH + 3 * ATTN_WIDTH
RMS_EPS = 1e-6

LANES = 128
HALO = 16
PROJ_ROWS = 1024
PROJ_SUB = 256
MERGE_ROWS = 1024
MERGE_SUB = 256
ATTN_ROWS = 2048
SUB = 64
WIN = 256
HALF = WIN // 2
GROUP = 16
SKEW = 8
LOG_WEIGHT_FLOOR = -110.0
LOG2_E = 1.4426950408889634
MASKED_LOGIT = -1e30
VMEM_LIMIT = 48 * 1024 * 1024

F32 = jnp.float32
BF16 = jnp.bfloat16


def _rms_norm(x, g):
    ms = jnp.mean(x * x, axis=-1, keepdims=True)
    return x * lax.rsqrt(ms + RMS_EPS) * g


def _silu(z):
    return z * jax.nn.sigmoid(z)


def _dot(a, b):
    return jnp.dot(a, b, preferred_element_type=F32)


def _proj_pool_kernel(x_ref, g_ref, w_ref, pw_ref, ps_ref,
                      yp_ref, q_ref, k_ref, v_ref, ext_ref):
    s = pl.program_id(1)
    rows = x_ref.shape[0]
    pw, aw = POOL_WIDTH, ATTN_WIDTH

    @pl.when(s == 0)
    def _():
        ext_ref[0:HALO, :] = jnp.zeros((HALO, pw), F32)

    for r0 in range(0, rows, PROJ_SUB):
        sub = slice(r0, r0 + PROJ_SUB)
        h = _rms_norm(x_ref[sub, :], g_ref[...]).astype(BF16)
        ext_ref[HALO + r0:HALO + r0 + PROJ_SUB, :] = _dot(h, w_ref[:, 0:pw])
        z = _dot(h, w_ref[:, pw:2 * pw])
        q_ref[sub, :] = (_dot(h, w_ref[:, 2 * pw:2 * pw + aw]) * HEAD_DIM ** -0.5).astype(BF16)
        k_ref[sub, :] = _dot(h, w_ref[:, 2 * pw + aw:2 * pw + 2 * aw]).astype(BF16)
        v_ref[sub, :] = _dot(h, w_ref[:, 2 * pw + 2 * aw:2 * pw + 3 * aw]).astype(BF16)

        head_pos = s * rows + r0 + lax.broadcasted_iota(jnp.int32, (HALO, 1), 0)
        for g, w in enumerate(POOL_WINDOWS):
            cols = slice(g * POOL_GROUP, (g + 1) * POOL_GROUP)
            e = ext_ref[r0:r0 + HALO + PROJ_SUB, cols]
            acc, span = e, 1
            while span < w:
                acc = acc + pltpu.roll(acc, span, axis=0)
                span *= 2
            ug, acc = e[HALO:], acc[HALO:]
            cnt = jnp.minimum(head_pos + 1, w).astype(F32)
            pooled = jnp.concatenate([acc[:HALO] / cnt, acc[HALO:] * (1.0 / w)], axis=0) - ug
            mixed = _dot(pooled.astype(BF16), pw_ref[g])
            y = mixed * ps_ref[:, cols] * _silu(z[:, cols])
            yp_ref[sub, cols] = y.astype(BF16)
    ext_ref[0:HALO, :] = ext_ref[rows:rows + HALO, :]


def _layer_spec(layer, shape, col_block=0):
    index = (layer,) + (0,) * (len(shape) - 1) + (col_block,)
    return pl.BlockSpec((None,) + tuple(shape), lambda b, s: index, pipeline_mode=pl.Buffered(1))


def _proj_pool(layer, x, norm_g, w_in, pool_w, pool_scale):
    B, S, D = x.shape
    rows = PROJ_ROWS
    act = jax.ShapeDtypeStruct((B, S, ATTN_WIDTH), BF16)
    row_spec = lambda width: pl.BlockSpec((None, rows, width), lambda b, s: (b, s, 0))
    return pl.pallas_call(
        _proj_pool_kernel,
        grid=(B, S // rows),
        in_specs=[row_spec(D), _layer_spec(layer, (1, D)), _layer_spec(layer, (D, PRE_ATTN_COLS), 0),
                  _layer_spec(layer, (len(POOL_WINDOWS), POOL_GROUP, POOL_GROUP)),
                  _layer_spec(layer, (1, POOL_WIDTH))],
        out_specs=[row_spec(POOL_WIDTH), row_spec(ATTN_WIDTH), row_spec(ATTN_WIDTH), row_spec(ATTN_WIDTH)],
        out_shape=[jax.ShapeDtypeStruct((B, S, POOL_WIDTH), BF16), act, act, act],
        scratch_shapes=[pltpu.VMEM((rows + HALO, POOL_WIDTH), F32)],
        compiler_params=pltpu.CompilerParams(
            dimension_semantics=("arbitrary", "arbitrary"), vmem_limit_bytes=VMEM_LIMIT),
        name="proj_pool",
    )(x, norm_g, w_in, pool_w, pool_scale)


def _stack_heads(q2):
    lane = lax.broadcasted_iota(jnp.int32, q2.shape, 1)
    zero = jnp.zeros_like(q2)
    return jnp.concatenate([jnp.where(lane < HEAD_DIM, q2, zero),
                            jnp.where(lane >= HEAD_DIM, q2, zero)], axis=0)


def _unstack_heads(acc):
    lane = lax.broadcasted_iota(jnp.int32, (SUB, LANES), 1)
    return jnp.where(lane < HEAD_DIM, acc[:SUB], acc[SUB:])


def _windows(blocks, sufw):
    n = len(blocks)
    logits, log_betas, sums, out = [None] * n, [None] * n, [None] * n, [None] * n

    def stage_logits(i):
        qs_fn, k_fn = blocks[i][0], blocks[i][1]
        logits[i] = lax.dot_general(qs_fn(), k_fn(), (((1,), (1,)), ((), ())),
                                    preferred_element_type=F32)

    def stage_suffix_sums(i):
        _, _, _, bias_a, bias_b, _ = blocks[i]
        la, lb = logits[i][:, :HALF], logits[i][:, HALF:]
        if bias_a is not None:
            la = la + bias_a
        if bias_b is not None:
            lb = lb + bias_b
        lg = jnp.concatenate([la, lb], axis=1)
        softplus = jnp.log(1.0 + jnp.exp2(jnp.abs(lg) * (-LOG2_E)))
        log_beta = jnp.minimum(lg, 0.0) - softplus
        l1m = log_beta - lg
        logits[i], log_betas[i] = None, log_beta
        sums[i] = (_dot(l1m.astype(BF16), sufw), jnp.sum(l1m, axis=1, keepdims=True))

    def stage_weights(i):
        (later, total), carry = sums[i], blocks[i][5]
        if carry is None:
            new_carry = total
        else:
            later = later + carry
            new_carry = carry + total
        p = jnp.exp(log_betas[i] + later).astype(BF16)
        sums[i], log_betas[i] = None, None
        out[i] = (_dot(p, blocks[i][2]()), new_carry)

    for i in range(n + 2 * SKEW):
        if i < n:
            stage_logits(i)
        if 0 <= i - SKEW < n:
            stage_suffix_sums(i - SKEW)
        if 0 <= i - 2 * SKEW < n:
            stage_weights(i - 2 * SKEW)
    return out


def _attn_kernel(q_ref, k_ref, v_ref, sufw_ref, o_ref, acc_ref, carry_ref, worst_ref):
    step = pl.program_id(2)
    row_in_sub = lax.broadcasted_iota(jnp.int32, (2 * SUB, HALF), 0) & (SUB - 1)
    col_a = lax.broadcasted_iota(jnp.int32, (2 * SUB, HALF), 1)
    col_b = col_a + HALF

    def masked(key_offset_limit):
        return (jnp.where(col_a < key_offset_limit, 0.0, MASKED_LOGIT),
                jnp.where(col_b < key_offset_limit, 0.0, MASKED_LOGIT))

    _, diag_bias = masked((WIN - SUB) + row_in_sub)

    def block(row, start, bias_a, bias_b, carry):
        keys = pl.ds(pl.multiple_of(start, SUB), WIN)
        return (lambda: _stack_heads(q_ref[pl.ds(row, SUB), :]),
                lambda: k_ref[keys, :], lambda: v_ref[keys, :], bias_a, bias_b, carry)

    for base_row in range(0, ATTN_ROWS, GROUP * SUB):
        base_q = step * ATTN_ROWS + base_row

        blocks = []
        for g in range(GROUP):
            row = base_row + g * SUB
            q0 = base_q + g * SUB
            if row < WIN - SUB:
                start = jnp.maximum(q0 - (WIN - SUB), 0)
                blocks.append(block(row, start, *masked((q0 - start) + row_in_sub), None))
            else:
                blocks.append(block(row, q0 - (WIN - SUB), None, diag_bias, None))
        worst = None
        for g, (pv, total) in enumerate(_windows(blocks, sufw_ref[...])):
            o_ref[pl.ds(base_row + g * SUB, SUB), :] = _unstack_heads(pv).astype(BF16)
            worst = total if worst is None else jnp.maximum(worst, total)
        worst_ref[0] = jnp.max(worst)

        @pl.when(worst_ref[0] > LOG_WEIGHT_FLOOR)
        def _():
            def sweep_block(g, _):
                row = pl.multiple_of(base_row + g * SUB, SUB)
                q0 = base_q + g * SUB

                def window(start, limit, carry):
                    (pv, total), = _windows([block(row, start, *masked(limit), carry)], sufw_ref[...])
                    return pv, total

                start0 = jnp.maximum(q0 - (WIN - SUB), 0)
                pv, total = window(start0, (q0 - start0) + row_in_sub, None)
                acc_ref[...] = pv
                carry_ref[...] = total

                def cond(c):
                    start, m = c
                    return jnp.logical_and(start > 0, m > LOG_WEIGHT_FLOOR)

                def body(c):
                    start, _ = c
                    nxt = jnp.maximum(start - WIN, 0)
                    pv, total = window(nxt, start - nxt, carry_ref[...])
                    acc_ref[...] = acc_ref[...] + pv
                    carry_ref[...] = total
                    return nxt, jnp.max(total)

                lax.while_loop(cond, body, (start0, jnp.max(total)))
                o_ref[pl.ds(row, SUB), :] = _unstack_heads(acc_ref[...]).astype(BF16)
                return 0

            lax.fori_loop(0, GROUP, sweep_block, 0)


def _suffix_sum_weights():
    j = jnp.arange(WIN)[:, None]
    c = jnp.arange(WIN)[None, :]
    return (j > c).astype(BF16)


def _attention(q, k, v):
    B, S, _ = q.shape
    n_pairs = ATTN_WIDTH // LANES
    q_spec = pl.BlockSpec((None, ATTN_ROWS, LANES), lambda b, p, s: (b, s, p))
    kv_spec = pl.BlockSpec((None, S, LANES), lambda b, p, s: (b, 0, p))
    return pl.pallas_call(
        _attn_kernel,
        grid=(B, n_pairs, S // ATTN_ROWS),
        in_specs=[q_spec, kv_spec, kv_spec, pl.BlockSpec((WIN, WIN), lambda b, p, s: (0, 0))],
        out_specs=q_spec,
        out_shape=jax.ShapeDtypeStruct((B, S, ATTN_WIDTH), BF16),
        scratch_shapes=[pltpu.VMEM((2 * SUB, LANES), F32), pltpu.VMEM((2 * SUB, 1), F32),
                        pltpu.SMEM((1,), F32)],
        compiler_params=pltpu.CompilerParams(
            dimension_semantics=("arbitrary", "arbitrary", "arbitrary"), vmem_limit_bytes=VMEM_LIMIT),
        name="stickbreak_attn",
    )(q, k, v, _suffix_sum_weights())


def _merge_kernel(x_ref, g_ref, w_ref, b_ref, yp_ref, at_ref, wpu_ref, wau_ref, wo_ref, fg_ref,
                  o_ref, *, final):
    aw, d = ATTN_WIDTH, D_MODEL
    for r0 in range(0, x_ref.shape[0], MERGE_SUB):
        sub = slice(r0, r0 + MERGE_SUB)
        x = x_ref[sub, :]
        h = _rms_norm(x, g_ref[...]).astype(BF16)
        z_attn = _dot(h, w_ref[:, 0:aw])
        y_attn = (at_ref[sub, :].astype(F32) * _silu(z_attn)).astype(BF16)
        gate_pool = jax.nn.sigmoid(_dot(h, w_ref[:, aw:aw + d]) + b_ref[:, 0:d])
        gate_attn = jax.nn.sigmoid(_dot(h, w_ref[:, aw + d:aw + 2 * d]) + b_ref[:, d:2 * d])
        merged = gate_pool * _dot(yp_ref[sub, :], wpu_ref[...]) + gate_attn * _dot(y_attn, wau_ref[...])
        out = x + _dot(merged.astype(BF16), wo_ref[...])
        if final:
            out = _rms_norm(out, fg_ref[...])
        o_ref[sub, :] = out


def _merge(layer, x, norm_g, w_in, b_gate, y_pool, attn, w_pool_up, w_attn_up, w_out, final_g, final):
    B, S, D = x.shape
    rows = MERGE_ROWS
    row_spec = lambda width: pl.BlockSpec((None, rows, width), lambda b, s: (b, s, 0))
    return pl.pallas_call(
        functools.partial(_merge_kernel, final=final),
        grid=(B, S // rows),
        in_specs=[row_spec(D), _layer_spec(layer, (1, D)), _layer_spec(layer, (D, PRE_ATTN_COLS), 1),
                  _layer_spec(layer, (1, 2 * D)), row_spec(POOL_WIDTH), row_spec(ATTN_WIDTH),
                  _layer_spec(layer, (POOL_WIDTH, D)), _layer_spec(layer, (ATTN_WIDTH, D)), _layer_spec(layer, (D, D)),
                  pl.BlockSpec((1, D), lambda b, s: (0, 0), pipeline_mode=pl.Buffered(1))],
        out_specs=row_spec(D),
        out_shape=jax.ShapeDtypeStruct((B, S, D), F32),
        compiler_params=pltpu.CompilerParams(
            dimension_semantics=("arbitrary", "arbitrary"), vmem_limit_bytes=VMEM_LIMIT),
        name="merge_out",
    )(x, norm_g, w_in, b_gate, y_pool, attn, w_pool_up, w_attn_up, w_out, final_g)


def kernel(x, norm_g, w_in, b_gate, pool_w, pool_scale, w_pool_up, w_attn_up, w_out, final_g):
    depth = norm_g.shape[0]
    w_in, pool_w, w_pool_up, w_attn_up, w_out = (a.astype(BF16) for a in (w_in, pool_w, w_pool_up, w_attn_up, w_out))
    norm_g = norm_g.reshape(depth, 1, D_MODEL)
    b_gate = b_gate.reshape(depth, 1, 2 * D_MODEL)
    pool_scale = pool_scale.reshape(depth, 1, POOL_WIDTH)
    final_g = final_g.reshape(1, D_MODEL)
    for l in range(depth):
        y_pool, q, k, v = _proj_pool(l, x, norm_g, w_in, pool_w, pool_scale)
        attn = _attention(q, k, v)
        x = _merge(l, x, norm_g, w_in, b_gate, y_pool, attn, w_pool_up, w_attn_up, w_out, final_g,
                   final=(l == depth - 1))
    return x
```

```python
import functools

import jax
import jax.numpy as jnp
from jax import lax
from jax.experimental import pallas as pl
from jax.experimental.pallas import tpu as pltpu

D_MODEL = 1024
POOL_WIDTH = 512
POOL_WINDOWS = (2, 4, 8, 16)
POOL_GROUP = 128
N_HEADS = 8
HEAD_DIM = 64
ATTN_WIDTH = N_HEADS * HEAD_DIM
PRE_ATTN_COLS = 2 * POOL_WIDTH + 3 * ATTN_WIDTH
RMS_EPS = 1e-6

LANES = 128
HALO = 16
PROJ_ROWS = 1024
PROJ_SUB = 256
MERGE_ROWS = 1024
MERGE_SUB = 256
ATTN_ROWS = 2048
SUB = 64
WIN = 256
HALF = WIN // 2
GROUP = 16
SKEW = 8
LOG_WEIGHT_FLOOR = -110.0
LOG2_E = 1.4426950408889634
MASKED_LOGIT = -1e30
VMEM_LIMIT = 48 * 1024 * 1024

F32 = jnp.float32
BF16 = jnp.bfloat16


def _rms_norm(x, g):
    ms = jnp.mean(x * x, axis=-1, keepdims=True)
    return x * lax.rsqrt(ms + RMS_EPS) * g


def _silu(z):
    return z * jax.nn.sigmoid(z)


def _dot(a, b):
    return jnp.dot(a, b, preferred_element_type=F32)


def _proj_pool_kernel(x_ref, g_ref, w_ref, pw_ref, ps_ref,
                      yp_ref, q_ref, k_ref, v_ref, ext_ref):
    s = pl.program_id(1)
    rows = x_ref.shape[0]
    pw, aw = POOL_WIDTH, ATTN_WIDTH

    @pl.when(s == 0)
    def _():
        ext_ref[0:HALO, :] = jnp.zeros((HALO, pw), F32)

    for r0 in range(0, rows, PROJ_SUB):
        sub = slice(r0, r0 + PROJ_SUB)
        h = _rms_norm(x_ref[sub, :], g_ref[...]).astype(BF16)
        ext_ref[HALO + r0:HALO + r0 + PROJ_SUB, :] = _dot(h, w_ref[:, 0:pw])
        z = _dot(h, w_ref[:, pw:2 * pw])
        q_ref[sub, :] = (_dot(h, w_ref[:, 2 * pw:2 * pw + aw]) * HEAD_DIM ** -0.5).astype(BF16)
        k_ref[sub, :] = _dot(h, w_ref[:, 2 * pw + aw:2 * pw + 2 * aw]).astype(BF16)
        v_ref[sub, :] = _dot(h, w_ref[:, 2 * pw + 2 * aw:2 * pw + 3 * aw]).astype(BF16)

        head_pos = s * rows + r0 + lax.broadcasted_iota(jnp.int32, (HALO, 1), 0)
        for g, w in enumerate(POOL_WINDOWS):
            cols = slice(g * POOL_GROUP, (g + 1) * POOL_GROUP)
            e = ext_ref[r0:r0 + HALO + PROJ_SUB, cols]
            acc, span = e, 1
            while span < w:
                acc = acc + pltpu.roll(acc, span, axis=0)
                span *= 2
            ug, acc = e[HALO:], acc[HALO:]
            cnt = jnp.minimum(head_pos + 1, w).astype(F32)
            pooled = jnp.concatenate([acc[:HALO] / cnt, acc[HALO:] * (1.0 / w)], axis=0) - ug
            mixed = _dot(pooled.astype(BF16), pw_ref[g])
            y = mixed * ps_ref[:, cols] * _silu(z[:, cols])
            yp_ref[sub, cols] = y.astype(BF16)
    ext_ref[0:HALO, :] = ext_ref[rows:rows + HALO, :]


def _layer_spec(layer, shape, col_block=0):
    index = (layer,) + (0,) * (len(shape) - 1) + (col_block,)
    return pl.BlockSpec((None,) + tuple(shape), lambda b, s: index, pipeline_mode=pl.Buffered(1))


def _proj_pool(layer, x, norm_g, w_in, pool_w, pool_scale):
    B, S, D = x.shape
    rows = PROJ_ROWS
    act = jax.ShapeDtypeStruct((B, S, ATTN_WIDTH), BF16)
    row_spec = lambda width: pl.BlockSpec((None, rows, width), lambda b, s: (b, s, 0))
    return pl.pallas_call(
        _proj_pool_kernel,
        grid=(B, S // rows),
        in_specs=[row_spec(D), _layer_spec(layer, (1, D)), _layer_spec(layer, (D, PRE_ATTN_COLS), 0),
                  _layer_spec(layer, (len(POOL_WINDOWS), POOL_GROUP, POOL_GROUP)),
                  _layer_spec(layer, (1, POOL_WIDTH))],
        out_specs=[row_spec(POOL_WIDTH), row_spec(ATTN_WIDTH), row_spec(ATTN_WIDTH), row_spec(ATTN_WIDTH)],
        out_shape=[jax.ShapeDtypeStruct((B, S, POOL_WIDTH), BF16), act, act, act],
        scratch_shapes=[pltpu.VMEM((rows + HALO, POOL_WIDTH), F32)],
        compiler_params=pltpu.CompilerParams(
            dimension_semantics=("arbitrary", "arbitrary"), vmem_limit_bytes=VMEM_LIMIT),
        name="proj_pool",
    )(x, norm_g, w_in, pool_w, pool_scale)


def _stack_heads(q2):
    lane = lax.broadcasted_iota(jnp.int32, q2.shape, 1)
    zero = jnp.zeros_like(q2)
    return jnp.concatenate([jnp.where(lane < HEAD_DIM, q2, zero),
                            jnp.where(lane >= HEAD_DIM, q2, zero)], axis=0)


def _unstack_heads(acc):
    lane = lax.broadcasted_iota(jnp.int32, (SUB, LANES), 1)
    return jnp.where(lane < HEAD_DIM, acc[:SUB], acc[SUB:])


def _windows(blocks, sufw):
    n = len(blocks)
    logits, log_betas, sums, out = [None] * n, [None] * n, [None] * n, [None] * n

    def stage_logits(i):
        qs_fn, k_fn = blocks[i][0], blocks[i][1]
        logits[i] = lax.dot_general(qs_fn(), k_fn(), (((1,), (1,)), ((), ())),
                                    preferred_element_type=F32)

    def stage_suffix_sums(i):
        _, _, _, bias_a, bias_b, _ = blocks[i]
        la, lb = logits[i][:, :HALF], logits[i][:, HALF:]
        if bias_a is not None:
            la = la + bias_a
        if bias_b is not None:
            lb = lb + bias_b
        lg = jnp.concatenate([la, lb], axis=1)
        softplus = jnp.log(1.0 + jnp.exp2(jnp.abs(lg) * (-LOG2_E)))
        log_beta = jnp.minimum(lg, 0.0) - softplus
        l1m = log_beta - lg
        logits[i], log_betas[i] = None, log_beta
        sums[i] = (_dot(l1m.astype(BF16), sufw), jnp.sum(l1m, axis=1, keepdims=True))

    def stage_weights(i):
        (later, total), carry = sums[i], blocks[i][5]
        if carry is None:
            new_carry = total
        else:
            later = later + carry
            new_carry = carry + total
        p = jnp.exp(log_betas[i] + later).astype(BF16)
        sums[i], log_betas[i] = None, None
        out[i] = (_dot(p, blocks[i][2]()), new_carry)

    for i in range(n + 2 * SKEW):
        if i < n:
            stage_logits(i)
        if 0 <= i - SKEW < n:
            stage_suffix_sums(i - SKEW)
        if 0 <= i - 2 * SKEW < n:
            stage_weights(i - 2 * SKEW)
    return out


def _attn_kernel(q_ref, k_ref, v_ref, sufw_ref, o_ref, acc_ref, carry_ref, worst_ref):
    step = pl.program_id(2)
    row_in_sub = lax.broadcasted_iota(jnp.int32, (2 * SUB, HALF), 0) & (SUB - 1)
    col_a = lax.broadcasted_iota(jnp.int32, (2 * SUB, HALF), 1)
    col_b = col_a + HALF

    def masked(key_offset_limit):
        return (jnp.where(col_a < key_offset_limit, 0.0, MASKED_LOGIT),
                jnp.where(col_b < key_offset_limit, 0.0, MASKED_LOGIT))

    _, diag_bias = masked((WIN - SUB) + row_in_sub)

    def block(row, start, bias_a, bias_b, carry):
        keys = pl.ds(pl.multiple_of(start, SUB), WIN)
        return (lambda: _stack_heads(q_ref[pl.ds(row, SUB), :]),
                lambda: k_ref[keys, :], lambda: v_ref[keys, :], bias_a, bias_b, carry)

    for base_row in range(0, ATTN_ROWS, GROUP * SUB):
        base_q = step * ATTN_ROWS + base_row

        blocks, starts = [], []
        for g in range(GROUP):
            row = base_row + g * SUB
            q0 = base_q + g * SUB
            if row < WIN - SUB:
                start = jnp.maximum(q0 - (WIN - SUB), 0)
                blocks.append(block(row, start, *masked((q0 - start) + row_in_sub), None))
            else:
                start = q0 - (WIN - SUB)
                blocks.append(block(row, start, None, diag_bias, None))
            starts.append(start)
        worst = None
        for g, (pv, total) in enumerate(_windows(blocks, sufw_ref[...])):
            o_ref[pl.ds(base_row + g * SUB, SUB), :] = _unstack_heads(pv).astype(BF16)
            if base_row + g * SUB <= WIN - SUB:
                total = jnp.where(starts[g] > 0, total, -jnp.inf)
            worst = total if worst is None else jnp.maximum(worst, total)
        worst_ref[0] = jnp.max(worst)

        @pl.when(worst_ref[0] > LOG_WEIGHT_FLOOR)
        def _():
            def sweep_block(g, _):
                row = pl.multiple_of(base_row + g * SUB, SUB)
                q0 = base_q + g * SUB

                def window(start, limit, carry):
                    (pv, total), = _windows([block(row, start, *masked(limit), carry)], sufw_ref[...])
                    return pv, total

                start0 = jnp.maximum(q0 - (WIN - SUB), 0)
                pv, total = window(start0, (q0 - start0) + row_in_sub, None)
                acc_ref[...] = pv
                carry_ref[...] = total

                def cond(c):
                    start, m = c
                    return jnp.logical_and(start > 0, m > LOG_WEIGHT_FLOOR)

                def body(c):
                    start, _ = c
                    nxt = jnp.maximum(start - WIN, 0)
                    pv, total = window(nxt, start - nxt, carry_ref[...])
                    acc_ref[...] = acc_ref[...] + pv
                    carry_ref[...] = total
                    return nxt, jnp.max(total)

                lax.while_loop(cond, body, (start0, jnp.max(total)))
                o_ref[pl.ds(row, SUB), :] = _unstack_heads(acc_ref[...]).astype(BF16)
                return 0

            lax.fori_loop(0, GROUP, sweep_block, 0)


def _suffix_sum_weights():
    j = jnp.arange(WIN)[:, None]
    c = jnp.arange(WIN)[None, :]
    return (j > c).astype(BF16)


def _attention(q, k, v):
    B, S, _ = q.shape
    n_pairs = ATTN_WIDTH // LANES
    q_spec = pl.BlockSpec((None, ATTN_ROWS, LANES), lambda b, p, s: (b, s, p))
    kv_spec = pl.BlockSpec((None, S, LANES), lambda b, p, s: (b, 0, p))
    return pl.pallas_call(
        _attn_kernel,
        grid=(B, n_pairs, S // ATTN_ROWS),
        in_specs=[q_spec, kv_spec, kv_spec, pl.BlockSpec((WIN, WIN), lambda b, p, s: (0, 0))],
        out_specs=q_spec,
        out_shape=jax.ShapeDtypeStruct((B, S, ATTN_WIDTH), BF16),
        scratch_shapes=[pltpu.VMEM((2 * SUB, LANES), F32), pltpu.VMEM((2 * SUB, 1), F32),
                        pltpu.SMEM((1,), F32)],
        compiler_params=pltpu.CompilerParams(
            dimension_semantics=("arbitrary", "arbitrary", "arbitrary"), vmem_limit_bytes=VMEM_LIMIT),
        name="stickbreak_attn",
    )(q, k, v, _suffix_sum_weights())


def _merge_kernel(x_ref, g_ref, w_ref, b_ref, yp_ref, at_ref, wpu_ref, wau_ref, wo_ref, fg_ref,
                  o_ref, *, final):
    aw, d = ATTN_WIDTH, D_MODEL
    for r0 in range(0, x_ref.shape[0], MERGE_SUB):
        sub = slice(r0, r0 + MERGE_SUB)
        x = x_ref[sub, :]
        h = _rms_norm(x, g_ref[...]).astype(BF16)
        z_attn = _dot(h, w_ref[:, 0:aw])
        y_attn = (at_ref[sub, :].astype(F32) * _silu(z_attn)).astype(BF16)
        gate_pool = jax.nn.sigmoid(_dot(h, w_ref[:, aw:aw + d]) + b_ref[:, 0:d])
        gate_attn = jax.nn.sigmoid(_dot(h, w_ref[:, aw + d:aw + 2 * d]) + b_ref[:, d:2 * d])
        merged = gate_pool * _dot(yp_ref[sub, :], wpu_ref[...]) + gate_attn * _dot(y_attn, wau_ref[...])
        out = x + _dot(merged.astype(BF16), wo_ref[...])
        if final:
            out = _rms_norm(out, fg_ref[...])
        o_ref[sub, :] = out


def _merge(layer, x, norm_g, w_in, b_gate, y_pool, attn, w_pool_up, w_attn_up, w_out, final_g, final):
    B, S, D = x.shape
    rows = MERGE_ROWS
    row_spec = lambda width: pl.BlockSpec((None, rows, width), lambda b, s: (b, s, 0))
    return pl.pallas_call(
        functools.partial(_merge_kernel, final=final),
        grid=(B, S // rows),
        in_specs=[row_spec(D), _layer_spec(layer, (1, D)), _layer_spec(layer, (D, PRE_ATTN_COLS), 1),
                  _layer_spec(layer, (1, 2 * D)), row_spec(POOL_WIDTH), row_spec(ATTN_WIDTH),
                  _layer_spec(layer, (POOL_WIDTH, D)), _layer_spec(layer, (ATTN_WIDTH, D)), _layer_spec(layer, (D, D)),
                  pl.BlockSpec((1, D), lambda b, s: (0, 0), pipeline_mode=pl.Buffered(1))],
        out_specs=row_spec(D),
        out_shape=jax.ShapeDtypeStruct((B, S, D), F32),
        compiler_params=pltpu.CompilerParams(
            dimension_semantics=("arbitrary", "arbitrary"), vmem_limit_bytes=VMEM_LIMIT),
        name="merge_out",
    )(x, norm_g, w_in, b_gate, y_pool, attn, w_pool_up, w_attn_up, w_out, final_g)


def kernel(x, norm_g, w_in, b_gate, pool_w, pool_scale, w_pool_up, w_attn_up, w_out, final_g):
    depth = norm_g.shape[0]
    w_in, pool_w, w_pool_up, w_attn_up, w_out = (a.astype(BF16) for a in (w_in, pool_w, w_pool_up, w_attn_up, w_out))
    norm_g = norm_g.reshape(depth, 1, D_MODEL)
    b_gate = b_gate.reshape(depth, 1, 2 * D_MODEL)
    pool_scale = pool_scale.reshape(depth, 1, POOL_WIDTH)
    final_g = final_g.reshape(1, D_MODEL)
    for l in range(depth):
        y_pool, q, k, v = _proj_pool(l, x, norm_g, w_in, pool_w, pool_scale)
        attn = _attention(q, k, v)
        x = _merge(l, x, norm_g, w_in, b_gate, y_pool, attn, w_pool_up, w_attn_up, w_out, final_g,
                   final=(l == depth - 1))
    return x
```

```python
import functools

import jax
import jax.numpy as jnp
from jax import lax
from jax.experimental import pallas as pl
from jax.experimental.pallas import tpu as pltpu

D_MODEL = 1024
POOL_WIDTH = 512
POOL_WINDOWS = (2, 4, 8, 16)
POOL_GROUP = 128
N_HEADS = 8
HEAD_DIM = 64
ATTN_WIDTH = N_HEADS * HEAD_DIM
PRE_ATTN_COLS = 2 * POOL_WIDTH + 3 * ATTN_WIDTH
RMS_EPS = 1e-6

LANES = 128
N_PAIRS = ATTN_WIDTH // LANES
HALO = 16
FUSE_ROWS = 1024
FUSE_SUB = 256
MERGE_ROWS = 1024
MERGE_SUB = 256
SUB = 64
WIN = 256
HALF = WIN // 2
SKEW = 8
LOG_WEIGHT_FLOOR = -110.0
LOG2_E = 1.4426950408889634
MASKED_LOGIT = -1e30
VMEM_LIMIT = 52 * 1024 * 1024

F32 = jnp.float32
BF16 = jnp.bfloat16


def _rms_norm(x, g):
    ms = jnp.mean(x * x, axis=-1, keepdims=True)
    return x * lax.rsqrt(ms + RMS_EPS) * g


def _silu(z):
    return z * jax.nn.sigmoid(z)


def _dot(a, b):
    return jnp.dot(a, b, preferred_element_type=F32)


def _stack_heads(q2):
    lane = lax.broadcasted_iota(jnp.int32, q2.shape, 1)
    zero = jnp.zeros_like(q2)
    return jnp.concatenate([jnp.where(lane < HEAD_DIM, q2, zero),
                            jnp.where(lane >= HEAD_DIM, q2, zero)], axis=0)


def _unstack_heads(acc):
    lane = lax.broadcasted_iota(jnp.int32, (SUB, LANES), 1)
    return jnp.where(lane < HEAD_DIM, acc[:SUB], acc[SUB:])


def _windows(blocks, sufw, fillers=()):
    n = len(blocks)
    logits, log_betas, sums, out = [None] * n, [None] * n, [None] * n, [None] * n

    def stage_logits(i):
        qs_fn, k_fn = blocks[i][0], blocks[i][1]
        logits[i] = lax.dot_general(qs_fn(), k_fn(), (((1,), (1,)), ((), ())),
                                    preferred_element_type=F32)

    def stage_suffix_sums(i):
        _, _, _, bias_a, bias_b, _ = blocks[i]
        la, lb = logits[i][:, :HALF], logits[i][:, HALF:]
        if bias_a is not None:
            la = la + bias_a
        if bias_b is not None:
            lb = lb + bias_b
        lg = jnp.concatenate([la, lb], axis=1)
        softplus = jnp.log(1.0 + jnp.exp2(jnp.abs(lg) * (-LOG2_E)))
        log_beta = jnp.minimum(lg, 0.0) - softplus
        l1m = log_beta - lg
        logits[i], log_betas[i] = None, log_beta
        sums[i] = (_dot(l1m.astype(BF16), sufw), jnp.sum(l1m, axis=1, keepdims=True))

    def stage_weights(i):
        (later, total), carry = sums[i], blocks[i][5]
        if carry is None:
            new_carry = total
        else:
            later = later + carry
            new_carry = carry + total
        p = jnp.exp(log_betas[i] + later).astype(BF16)
        sums[i], log_betas[i] = None, None
        out[i] = (_dot(p, blocks[i][2]()), new_carry)

    n_iter = n + 2 * SKEW
    fillers = list(fillers)
    issued = 0
    for i in range(n_iter):
        if i < n:
            stage_logits(i)
        if 0 <= i - SKEW < n:
            stage_suffix_sums(i - SKEW)
        if 0 <= i - 2 * SKEW < n:
            stage_weights(i - 2 * SKEW)
        while issued < len(fillers) and (issued + 1) * n_iter <= (i + 1) * (len(fillers) + 1):
            fillers[issued]()
            issued += 1
    for f in fillers[issued:]:
        f()
    return out


def _suffix_sum_weights():
    j = jnp.arange(WIN)[:, None]
    c = jnp.arange(WIN)[None, :]
    return (j > c).astype(BF16)


def _proj_attn_kernel(x_ref, g_ref, w_ref, pw_ref, ps_ref, sufw_ref, yp_ref, at_ref,
                      ext_ref, q_s, k_s, v_s, acc_ref, carry_ref, worst_ref):
    s = pl.program_id(1)
    rows = x_ref.shape[0]
    seq0 = s * rows
    pw, aw = POOL_WIDTH, ATTN_WIDTH

    row_in_sub = lax.broadcasted_iota(jnp.int32, (2 * SUB, HALF), 0) & (SUB - 1)
    col_a = lax.broadcasted_iota(jnp.int32, (2 * SUB, HALF), 1)
    col_b = col_a + HALF

    def masked(key_offset_limit):
        return (jnp.where(col_a < key_offset_limit, 0.0, MASKED_LOGIT),
                jnp.where(col_b < key_offset_limit, 0.0, MASKED_LOGIT))

    _, diag_bias = masked((WIN - SUB) + row_in_sub)

    @pl.when(s == 0)
    def _():
        ext_ref[0:HALO, :] = jnp.zeros((HALO, pw), F32)

    def projection_stages(r0):
        sub = slice(r0, r0 + FUSE_SUB)
        seq_rows = pl.ds(pl.multiple_of(seq0 + r0, FUSE_SUB), FUSE_SUB)
        st = {}

        def norm():
            st["h"] = _rms_norm(x_ref[sub, :], g_ref[...]).astype(BF16)

        def pool_values():
            ext_ref[HALO + r0:HALO + r0 + FUSE_SUB, :] = _dot(st["h"], w_ref[:, 0:pw])

        def pool_gate():
            st["z"] = _dot(st["h"], w_ref[:, pw:2 * pw])

        def queries():
            q_s[sub, :] = (_dot(st["h"], w_ref[:, 2 * pw:2 * pw + aw]) * HEAD_DIM ** -0.5).astype(BF16)

        def keys():
            k_s[seq_rows, :] = _dot(st["h"], w_ref[:, 2 * pw + aw:2 * pw + 2 * aw]).astype(BF16)

        def values():
            v_s[seq_rows, :] = _dot(st["h"], w_ref[:, 2 * pw + 2 * aw:2 * pw + 3 * aw]).astype(BF16)

        def pooling():
            head_pos = seq0 + r0 + lax.broadcasted_iota(jnp.int32, (HALO, 1), 0)
            for g, w in enumerate(POOL_WINDOWS):
                cols = slice(g * POOL_GROUP, (g + 1) * POOL_GROUP)
                e = ext_ref[r0:r0 + HALO + FUSE_SUB, cols]
                acc, span = e, 1
                while span < w:
                    acc = acc + pltpu.roll(acc, span, axis=0)
                    span *= 2
                ug, acc = e[HALO:], acc[HALO:]
                cnt = jnp.minimum(head_pos + 1, w).astype(F32)
                pooled = jnp.concatenate([acc[:HALO] / cnt, acc[HALO:] * (1.0 / w)], axis=0) - ug
                mixed = _dot(pooled.astype(BF16), pw_ref[g])
                y = mixed * ps_ref[:, cols] * _silu(st["z"][:, cols])
                yp_ref[sub, cols] = y.astype(BF16)

        return [norm, pool_values, pool_gate, queries, keys, values, pooling]

    def first_window_blocks(r0):
        blocks, slots = [], []
        for row in range(r0, r0 + FUSE_SUB, SUB):
            q0 = seq0 + row
            if row < WIN - SUB:
                start = jnp.maximum(q0 - (WIN - SUB), 0)
                bias_a, bias_b = masked((q0 - start) + row_in_sub)
            else:
                start = q0 - (WIN - SUB)
                bias_a, bias_b = None, diag_bias
            keys = pl.ds(pl.multiple_of(start, SUB), WIN)
            for p in range(N_PAIRS):
                lanes = slice(p * LANES, (p + 1) * LANES)
                blocks.append((functools.partial(lambda r, l: _stack_heads(q_s[r:r + SUB, l]), row, lanes),
                               functools.partial(lambda k, l: k_s[k, l], keys, lanes),
                               functools.partial(lambda k, l: v_s[k, l], keys, lanes),
                               bias_a, bias_b, None))
                slots.append((slice(row, row + SUB), lanes, start if row <= WIN - SUB else None))
        return blocks, slots

    for stage in projection_stages(0):
        stage()
    worst = None
    for r0 in range(0, rows, FUSE_SUB):
        fillers = projection_stages(r0 + FUSE_SUB) if r0 + FUSE_SUB < rows else ()
        blocks, slots = first_window_blocks(r0)
        for (pv, total), (out_rows, lanes, start) in zip(_windows(blocks, sufw_ref[...], fillers), slots):
            at_ref[out_rows, lanes] = _unstack_heads(pv).astype(BF16)
            if start is not None:
                total = jnp.where(start > 0, total, -jnp.inf)
            worst = total if worst is None else jnp.maximum(worst, total)
    worst_ref[0] = jnp.max(worst)
    ext_ref[0:HALO, :] = ext_ref[rows:rows + HALO, :]

    @pl.when(worst_ref[0] > LOG_WEIGHT_FLOOR)
    def _():
        def sweep_rows(blk, _):
            row = pl.multiple_of(blk * SUB, SUB)
            q0 = seq0 + row
            for p in range(N_PAIRS):
                lanes = slice(p * LANES, (p + 1) * LANES)
                qs_fn = lambda: _stack_heads(q_s[pl.ds(row, SUB), lanes])

                def window(start, limit, carry):
                    keys = pl.ds(pl.multiple_of(start, SUB), WIN)
                    block = (qs_fn, lambda: k_s[keys, lanes], lambda: v_s[keys, lanes]) + masked(limit) + (carry,)
                    (pv, total), = _windows([block], sufw_ref[...])
                    return pv, total

                start0 = jnp.maximum(q0 - (WIN - SUB), 0)
                pv, total = window(start0, (q0 - start0) + row_in_sub, None)
                acc_ref[...] = pv
                carry_ref[...] = total

                def cond(c):
                    start, m = c
                    return jnp.logical_and(start > 0, m > LOG_WEIGHT_FLOOR)

                def body(c):
                    start, _ = c
                    nxt = jnp.maximum(start - WIN, 0)
                    pv, total = window(nxt, start - nxt, carry_ref[...])
                    acc_ref[...] = acc_ref[...] + pv
                    carry_ref[...] = total
                    return nxt, jnp.max(total)

                lax.while_loop(cond, body, (start0, jnp.max(total)))
                at_ref[pl.ds(row, SUB), lanes] = _unstack_heads(acc_ref[...]).astype(BF16)
            return 0

        lax.fori_loop(0, rows // SUB, sweep_rows, 0)


def _layer_spec(layer, shape, col_block=0):
    index = (layer,) + (0,) * (len(shape) - 1) + (col_block,)
    return pl.BlockSpec((None,) + tuple(shape), lambda b, s: index, pipeline_mode=pl.Buffered(1))


def _proj_attn(layer, x, norm_g, w_in, pool_w, pool_scale):
    B, S, D = x.shape
    rows = FUSE_ROWS
    row_spec = lambda width: pl.BlockSpec((None, rows, width), lambda b, s: (b, s, 0))
    return pl.pallas_call(
        _proj_attn_kernel,
        grid=(B, S // rows),
        in_specs=[row_spec(D), _layer_spec(layer, (1, D)), _layer_spec(layer, (D, PRE_ATTN_COLS), 0),
                  _layer_spec(layer, (len(POOL_WINDOWS), POOL_GROUP, POOL_GROUP)),
                  _layer_spec(layer, (1, POOL_WIDTH)),
                  pl.BlockSpec((WIN, WIN), lambda b, s: (0, 0), pipeline_mode=pl.Buffered(1))],
        out_specs=[row_spec(POOL_WIDTH), row_spec(ATTN_WIDTH)],
        out_shape=[jax.ShapeDtypeStruct((B, S, POOL_WIDTH), BF16), jax.ShapeDtypeStruct((B, S, ATTN_WIDTH), BF16)],
        scratch_shapes=[pltpu.VMEM((rows + HALO, POOL_WIDTH), F32),
                        pltpu.VMEM((rows, ATTN_WIDTH), BF16),
                        pltpu.VMEM((S, ATTN_WIDTH), BF16),
                        pltpu.VMEM((S, ATTN_WIDTH), BF16),
                        pltpu.VMEM((2 * SUB, LANES), F32),
                        pltpu.VMEM((2 * SUB, 1), F32),
                        pltpu.SMEM((1,), F32)],
        compiler_params=pltpu.CompilerParams(
            dimension_semantics=("arbitrary", "arbitrary"), vmem_limit_bytes=VMEM_LIMIT),
        name="proj_attn",
    )(x, norm_g, w_in, pool_w, pool_scale, _suffix_sum_weights())


def _merge_kernel(x_ref, g_ref, w_ref, b_ref, yp_ref, at_ref, wpu_ref, wau_ref, wo_ref, fg_ref,
                  o_ref, *, final):
    aw, d = ATTN_WIDTH, D_MODEL
    for r0 in range(0, x_ref.shape[0], MERGE_SUB):
        sub = slice(r0, r0 + MERGE_SUB)
        x = x_ref[sub, :]
        h = _rms_norm(x, g_ref[...]).astype(BF16)
        z_attn = _dot(h, w_ref[:, 0:aw])
        y_attn = (at_ref[sub, :].astype(F32) * _silu(z_attn)).astype(BF16)
        gate_pool = jax.nn.sigmoid(_dot(h, w_ref[:, aw:aw + d]) + b_ref[:, 0:d])
        gate_attn = jax.nn.sigmoid(_dot(h, w_ref[:, aw + d:aw + 2 * d]) + b_ref[:, d:2 * d])
        merged = gate_pool * _dot(yp_ref[sub, :], wpu_ref[...]) + gate_attn * _dot(y_attn, wau_ref[...])
        out = x + _dot(merged.astype(BF16), wo_ref[...])
        if final:
            out = _rms_norm(out, fg_ref[...])
        o_ref[sub, :] = out


def _merge(layer, x, norm_g, w_in, b_gate, y_pool, attn, w_pool_up, w_attn_up, w_out, final_g, final):
    B, S, D = x.shape
    rows = MERGE_ROWS
    row_spec = lambda width: pl.BlockSpec((None, rows, width), lambda b, s: (b, s, 0))
    return pl.pallas_call(
        functools.partial(_merge_kernel, final=final),
        grid=(B, S // rows),
        in_specs=[row_spec(D), _layer_spec(layer, (1, D)), _layer_spec(layer, (D, PRE_ATTN_COLS), 1),
                  _layer_spec(layer, (1, 2 * D)), row_spec(POOL_WIDTH), row_spec(ATTN_WIDTH),
                  _layer_spec(layer, (POOL_WIDTH, D)), _layer_spec(layer, (ATTN_WIDTH, D)), _layer_spec(layer, (D, D)),
                  pl.BlockSpec((1, D), lambda b, s: (0, 0), pipeline_mode=pl.Buffered(1))],
        out_specs=row_spec(D),
        out_shape=jax.ShapeDtypeStruct((B, S, D), F32),
        compiler_params=pltpu.CompilerParams(
            dimension_semantics=("arbitrary", "arbitrary"), vmem_limit_bytes=VMEM_LIMIT),
        name="merge_out",
    )(x, norm_g, w_in, b_gate, y_pool, attn, w_pool_up, w_attn_up, w_out, final_g)


def kernel(x, norm_g, w_in, b_gate, pool_w, pool_scale, w_pool_up, w_attn_up, w_out, final_g):
    depth = norm_g.shape[0]
    w_in, pool_w, w_pool_up, w_attn_up, w_out = (a.astype(BF16) for a in (w_in, pool_w, w_pool_up, w_attn_up, w_out))
    norm_g = norm_g.reshape(depth, 1, D_MODEL)
    b_gate = b_gate.reshape(depth, 1, 2 * D_MODEL)
    pool_scale = pool_scale.reshape(depth, 1, POOL_WIDTH)
    final_g = final_g.reshape(1, D_MODEL)
    for l in range(depth):
        y_pool, attn = _proj_attn(l, x, norm_g, w_in, pool_w, pool_scale)
        x = _merge(l, x, norm_g, w_in, b_gate, y_pool, attn, w_pool_up, w_attn_up, w_out, final_g,
                   final=(l == depth - 1))
    return x
```

```python
import functools

import jax
import jax.numpy as jnp
from jax import lax
from jax.experimental import pallas as pl
from jax.experimental.pallas import tpu as pltpu

D_MODEL = 1024
POOL_WIDTH = 512
POOL_WINDOWS = (2, 4, 8, 16)
POOL_GROUP = 128
N_HEADS = 8
HEAD_DIM = 64
ATTN_WIDTH = N_HEADS * HEAD_DIM
PRE_ATTN_COLS = 2 * POOL_WIDTH + 3 * ATTN_WIDTH
RMS_EPS = 1e-6

LANES = 128
N_PAIRS = ATTN_WIDTH // LANES
HALO = 16
FUSE_ROWS = 1024
FUSE_SUB = 256
MERGE_ROWS = 1024
MERGE_SUB = 256
SUB = 64
WIN = 256
HALF = WIN // 2
SKEW = 4
LOG_WEIGHT_FLOOR = -110.0
LOG2_E = 1.4426950408889634
MASKED_LOGIT = -1e30
VMEM_LIMIT = 52 * 1024 * 1024

F32 = jnp.float32
BF16 = jnp.bfloat16


def _rms_norm(x, g):
    ms = jnp.mean(x * x, axis=-1, keepdims=True)
    return x * lax.rsqrt(ms + RMS_EPS) * g


def _silu(z):
    return z * jax.nn.sigmoid(z)


def _dot(a, b):
    return jnp.dot(a, b, preferred_element_type=F32)


def _stack_heads(q2):
    lane = lax.broadcasted_iota(jnp.int32, q2.shape, 1)
    zero = jnp.zeros_like(q2)
    return jnp.concatenate([jnp.where(lane < HEAD_DIM, q2, zero),
                            jnp.where(lane >= HEAD_DIM, q2, zero)], axis=0)


def _unstack_heads(acc):
    lane = lax.broadcasted_iota(jnp.int32, (SUB, LANES), 1)
    return jnp.where(lane < HEAD_DIM, acc[:SUB], acc[SUB:])


def _windows(blocks, sufw, fillers=()):
    n = len(blocks)
    logits, log_betas, sums, out = [None] * n, [None] * n, [None] * n, [None] * n

    def stage_logits(i):
        qs_fn, k_fn = blocks[i][0], blocks[i][1]
        logits[i] = lax.dot_general(qs_fn(), k_fn(), (((1,), (1,)), ((), ())),
                                    preferred_element_type=F32)

    def stage_suffix_sums(i):
        _, _, _, bias_a, bias_b, _ = blocks[i]
        la, lb = logits[i][:, :HALF], logits[i][:, HALF:]
        if bias_a is not None:
            la = la + bias_a
        if bias_b is not None:
            lb = lb + bias_b
        lg = jnp.concatenate([la, lb], axis=1)
        softplus = jnp.log(1.0 + jnp.exp2(jnp.abs(lg) * (-LOG2_E)))
        log_beta = jnp.minimum(lg, 0.0) - softplus
        l1m = log_beta - lg
        logits[i], log_betas[i] = None, log_beta
        sums[i] = (_dot(l1m.astype(BF16), sufw), jnp.sum(l1m, axis=1, keepdims=True))

    def stage_weights(i):
        (later, total), carry = sums[i], blocks[i][5]
        if carry is None:
            new_carry = total
        else:
            later = later + carry
            new_carry = carry + total
        p = jnp.exp(log_betas[i] + later).astype(BF16)
        sums[i], log_betas[i] = None, None
        out[i] = (_dot(p, blocks[i][2]()), new_carry)

    n_iter = n + 2 * SKEW
    fillers = list(fillers)
    issued = 0
    for i in range(n_iter):
        if i < n:
            stage_logits(i)
        if 0 <= i - SKEW < n:
            stage_suffix_sums(i - SKEW)
        if 0 <= i - 2 * SKEW < n:
            stage_weights(i - 2 * SKEW)
        while issued < len(fillers) and (issued + 1) * n_iter <= (i + 1) * (len(fillers) + 1):
            fillers[issued]()
            issued += 1
    for f in fillers[issued:]:
        f()
    return out


def _suffix_sum_weights():
    j = jnp.arange(WIN)[:, None]
    c = jnp.arange(WIN)[None, :]
    return (j > c).astype(BF16)


def _proj_attn_kernel(x_ref, g_ref, w_ref, pw_ref, ps_ref, sufw_ref, yp_ref, at_ref,
                      ext_ref, q_s, k_s, v_s, acc_ref, carry_ref, worst_ref):
    s = pl.program_id(1)
    rows = x_ref.shape[0]
    seq0 = s * rows
    pw, aw = POOL_WIDTH, ATTN_WIDTH

    row_in_sub = lax.broadcasted_iota(jnp.int32, (2 * SUB, HALF), 0) & (SUB - 1)
    col_a = lax.broadcasted_iota(jnp.int32, (2 * SUB, HALF), 1)
    col_b = col_a + HALF

    def masked(key_offset_limit):
        return (jnp.where(col_a < key_offset_limit, 0.0, MASKED_LOGIT),
                jnp.where(col_b < key_offset_limit, 0.0, MASKED_LOGIT))

    _, diag_bias = masked((WIN - SUB) + row_in_sub)

    @pl.when(s == 0)
    def _():
        ext_ref[0:HALO, :] = jnp.zeros((HALO, pw), F32)

    def projection_stages(r0):
        sub = slice(r0, r0 + FUSE_SUB)
        seq_rows = pl.ds(pl.multiple_of(seq0 + r0, FUSE_SUB), FUSE_SUB)
        st = {}

        def norm():
            st["h"] = _rms_norm(x_ref[sub, :], g_ref[...]).astype(BF16)

        def pool_values():
            ext_ref[HALO + r0:HALO + r0 + FUSE_SUB, :] = _dot(st["h"], w_ref[:, 0:pw])

        def pool_gate():
            st["z"] = _dot(st["h"], w_ref[:, pw:2 * pw])

        def queries():
            q_s[sub, :] = (_dot(st["h"], w_ref[:, 2 * pw:2 * pw + aw]) * HEAD_DIM ** -0.5).astype(BF16)

        def keys():
            k_s[seq_rows, :] = _dot(st["h"], w_ref[:, 2 * pw + aw:2 * pw + 2 * aw]).astype(BF16)

        def values():
            v_s[seq_rows, :] = _dot(st["h"], w_ref[:, 2 * pw + 2 * aw:2 * pw + 3 * aw]).astype(BF16)

        def pooling():
            head_pos = seq0 + r0 + lax.broadcasted_iota(jnp.int32, (HALO, 1), 0)
            for g, w in enumerate(POOL_WINDOWS):
                cols = slice(g * POOL_GROUP, (g + 1) * POOL_GROUP)
                e = ext_ref[r0:r0 + HALO + FUSE_SUB, cols]
                acc, span = e, 1
                while span < w:
                    acc = acc + pltpu.roll(acc, span, axis=0)
                    span *= 2
                ug, acc = e[HALO:], acc[HALO:]
                cnt = jnp.minimum(head_pos + 1, w).astype(F32)
                pooled = jnp.concatenate([acc[:HALO] / cnt, acc[HALO:] * (1.0 / w)], axis=0) - ug
                mixed = _dot(pooled.astype(BF16), pw_ref[g])
                y = mixed * ps_ref[:, cols] * _silu(st["z"][:, cols])
                yp_ref[sub, cols] = y.astype(BF16)

        return [norm, pool_values, pool_gate, queries, keys, values, pooling]

    def first_window_blocks(r0):
        blocks, slots = [], []
        for row in range(r0, r0 + FUSE_SUB, SUB):
            q0 = seq0 + row
            if row < WIN - SUB:
                start = jnp.maximum(q0 - (WIN - SUB), 0)
                bias_a, bias_b = masked((q0 - start) + row_in_sub)
            else:
                start = q0 - (WIN - SUB)
                bias_a, bias_b = None, diag_bias
            keys = pl.ds(pl.multiple_of(start, SUB), WIN)
            for p in range(N_PAIRS):
                lanes = slice(p * LANES, (p + 1) * LANES)
                blocks.append((functools.partial(lambda r, l: _stack_heads(q_s[r:r + SUB, l]), row, lanes),
                               functools.partial(lambda k, l: k_s[k, l], keys, lanes),
                               functools.partial(lambda k, l: v_s[k, l], keys, lanes),
                               bias_a, bias_b, None))
                slots.append((slice(row, row + SUB), lanes, start if row <= WIN - SUB else None))
        return blocks, slots

    for stage in projection_stages(0):
        stage()
    worst = None
    for r0 in range(0, rows, FUSE_SUB):
        fillers = projection_stages(r0 + FUSE_SUB) if r0 + FUSE_SUB < rows else ()
        blocks, slots = first_window_blocks(r0)
        for (pv, total), (out_rows, lanes, start) in zip(_windows(blocks, sufw_ref[...], fillers), slots):
            at_ref[out_rows, lanes] = _unstack_heads(pv).astype(BF16)
            if start is not None:
                total = jnp.where(start > 0, total, -jnp.inf)
            worst = total if worst is None else jnp.maximum(worst, total)
    worst_ref[0] = jnp.max(worst)
    ext_ref[0:HALO, :] = ext_ref[rows:rows + HALO, :]

    @pl.when(worst_ref[0] > LOG_WEIGHT_FLOOR)
    def _():
        def sweep_rows(blk, _):
            row = pl.multiple_of(blk * SUB, SUB)
            q0 = seq0 + row
            for p in range(N_PAIRS):
                lanes = slice(p * LANES, (p + 1) * LANES)
                qs_fn = lambda: _stack_heads(q_s[pl.ds(row, SUB), lanes])

                def window(start, limit, carry):
                    keys = pl.ds(pl.multiple_of(start, SUB), WIN)
                    block = (qs_fn, lambda: k_s[keys, lanes], lambda: v_s[keys, lanes]) + masked(limit) + (carry,)
                    (pv, total), = _windows([block], sufw_ref[...])
                    return pv, total

                start0 = jnp.maximum(q0 - (WIN - SUB), 0)
                pv, total = window(start0, (q0 - start0) + row_in_sub, None)
                acc_ref[...] = pv
                carry_ref[...] = total

                def cond(c):
                    start, m = c
                    return jnp.logical_and(start > 0, m > LOG_WEIGHT_FLOOR)

                def body(c):
                    start, _ = c
                    nxt = jnp.maximum(start - WIN, 0)
                    pv, total = window(nxt, start - nxt, carry_ref[...])
                    acc_ref[...] = acc_ref[...] + pv
                    carry_ref[...] = total
                    return nxt, jnp.max(total)

                lax.while_loop(cond, body, (start0, jnp.max(total)))
                at_ref[pl.ds(row, SUB), lanes] = _unstack_heads(acc_ref[...]).astype(BF16)
            return 0

        lax.fori_loop(0, rows // SUB, sweep_rows, 0)


def _layer_spec(layer, shape, col_block=0):
    index = (layer,) + (0,) * (len(shape) - 1) + (col_block,)
    return pl.BlockSpec((None,) + tuple(shape), lambda b, s: index, pipeline_mode=pl.Buffered(1))


def _proj_attn(layer, x, norm_g, w_in, pool_w, pool_scale):
    B, S, D = x.shape
    rows = FUSE_ROWS
    row_spec = lambda width: pl.BlockSpec((None, rows, width), lambda b, s: (b, s, 0))
    return pl.pallas_call(
        _proj_attn_kernel,
        grid=(B, S // rows),
        in_specs=[row_spec(D), _layer_spec(layer, (1, D)), _layer_spec(layer, (D, PRE_ATTN_COLS), 0),
                  _layer_spec(layer, (len(POOL_WINDOWS), POOL_GROUP, POOL_GROUP)),
                  _layer_spec(layer, (1, POOL_WIDTH)),
                  pl.BlockSpec((WIN, WIN), lambda b, s: (0, 0), pipeline_mode=pl.Buffered(1))],
        out_specs=[row_spec(POOL_WIDTH), row_spec(ATTN_WIDTH)],
        out_shape=[jax.ShapeDtypeStruct((B, S, POOL_WIDTH), BF16), jax.ShapeDtypeStruct((B, S, ATTN_WIDTH), BF16)],
        scratch_shapes=[pltpu.VMEM((rows + HALO, POOL_WIDTH), F32),
                        pltpu.VMEM((rows, ATTN_WIDTH), BF16),
                        pltpu.VMEM((S, ATTN_WIDTH), BF16),
                        pltpu.VMEM((S, ATTN_WIDTH), BF16),
                        pltpu.VMEM((2 * SUB, LANES), F32),
                        pltpu.VMEM((2 * SUB, 1), F32),
                        pltpu.SMEM((1,), F32)],
        compiler_params=pltpu.CompilerParams(
            dimension_semantics=("arbitrary", "arbitrary"), vmem_limit_bytes=VMEM_LIMIT),
        name="proj_attn",
    )(x, norm_g, w_in, pool_w, pool_scale, _suffix_sum_weights())


def _merge_kernel(x_ref, g_ref, w_ref, b_ref, yp_ref, at_ref, wpu_ref, wau_ref, wo_ref, fg_ref,
                  o_ref, *, final):
    aw, d = ATTN_WIDTH, D_MODEL
    for r0 in range(0, x_ref.shape[0], MERGE_SUB):
        sub = slice(r0, r0 + MERGE_SUB)
        x = x_ref[sub, :]
        h = _rms_norm(x, g_ref[...]).astype(BF16)
        z_attn = _dot(h, w_ref[:, 0:aw])
        y_attn = (at_ref[sub, :].astype(F32) * _silu(z_attn)).astype(BF16)
        gate_pool = jax.nn.sigmoid(_dot(h, w_ref[:, aw:aw + d]) + b_ref[:, 0:d])
        gate_attn = jax.nn.sigmoid(_dot(h, w_ref[:, aw + d:aw + 2 * d]) + b_ref[:, d:2 * d])
        merged = gate_pool * _dot(yp_ref[sub, :], wpu_ref[...]) + gate_attn * _dot(y_attn, wau_ref[...])
        out = x + _dot(merged.astype(BF16), wo_ref[...])
        if final:
            out = _rms_norm(out, fg_ref[...])
        o_ref[sub, :] = out


def _merge(layer, x, norm_g, w_in, b_gate, y_pool, attn, w_pool_up, w_attn_up, w_out, final_g, final):
    B, S, D = x.shape
    rows = MERGE_ROWS
    row_spec = lambda width: pl.BlockSpec((None, rows, width), lambda b, s: (b, s, 0))
    return pl.pallas_call(
        functools.partial(_merge_kernel, final=final),
        grid=(B, S // rows),
        in_specs=[row_spec(D), _layer_spec(layer, (1, D)), _layer_spec(layer, (D, PRE_ATTN_COLS), 1),
                  _layer_spec(layer, (1, 2 * D)), row_spec(POOL_WIDTH), row_spec(ATTN_WIDTH),
                  _layer_spec(layer, (POOL_WIDTH, D)), _layer_spec(layer, (ATTN_WIDTH, D)), _layer_spec(layer, (D, D)),
                  pl.BlockSpec((1, D), lambda b, s: (0, 0), pipeline_mode=pl.Buffered(1))],
        out_specs=row_spec(D),
        out_shape=jax.ShapeDtypeStruct((B, S, D), F32),
        compiler_params=pltpu.CompilerParams(
            dimension_semantics=("arbitrary", "arbitrary"), vmem_limit_bytes=VMEM_LIMIT),
        name="merge_out",
    )(x, norm_g, w_in, b_gate, y_pool, attn, w_pool_up, w_attn_up, w_out, final_g)


def kernel(x, norm_g, w_in, b_gate, pool_w, pool_scale, w_pool_up, w_attn_up, w_out, final_g):
    depth = norm_g.shape[0]
    w_in, pool_w, w_pool_up, w_attn_up, w_out = (a.astype(BF16) for a in (w_in, pool_w, w_pool_up, w_attn_up, w_out))
    norm_g = norm_g.reshape(depth, 1, D_MODEL)
    b_gate = b_gate.reshape(depth, 1, 2 * D_MODEL)
    pool_scale = pool_scale.reshape(depth, 1, POOL_WIDTH)
    final_g = final_g.reshape(1, D_MODEL)
    for l in range(depth):
        y_pool, attn = _proj_attn(l, x, norm_g, w_in, pool_w, pool_scale)
        x = _merge(l, x, norm_g, w_in, b_gate, y_pool, attn, w_pool_up, w_attn_up, w_out, final_g,
                   final=(l == depth - 1))
    return x
```

```python
import functools

import jax
import jax.numpy as jnp
from jax import lax
from jax.experimental import pallas as pl
from jax.experimental.pallas import tpu as pltpu

D_MODEL = 1024
POOL_WIDTH = 512
POOL_WINDOWS = (2, 4, 8, 16)
POOL_GROUP = 128
N_HEADS = 8
HEAD_DIM = 64
ATTN_WIDTH = N_HEADS * HEAD_DIM
PRE_ATTN_COLS = 2 * POOL_WIDTH + 3 * ATTN_WIDTH
RMS_EPS = 1e-6

LANES = 128
N_PAIRS = ATTN_WIDTH // LANES
HALO = 16
FUSE_ROWS = 1024
FUSE_SUB = 256
MERGE_ROWS = 1024
MERGE_SUB = 256
SUB = 64
WIN = 256
HALF = WIN // 2
SKEW = 8
LOG_WEIGHT_FLOOR = -110.0
LOG2_E = 1.4426950408889634
MASKED_LOGIT = -1e30
VMEM_LIMIT = 52 * 1024 * 1024

F32 = jnp.float32
BF16 = jnp.bfloat16


def _rms_norm(x, g):
    ms = jnp.mean(x * x, axis=-1, keepdims=True)
    return x * lax.rsqrt(ms + RMS_EPS) * g


def _silu(z):
    return z * jax.nn.sigmoid(z)


def _dot(a, b):
    return jnp.dot(a, b, preferred_element_type=F32)


def _stack_heads(q2):
    lane = lax.broadcasted_iota(jnp.int32, q2.shape, 1)
    zero = jnp.zeros_like(q2)
    return jnp.concatenate([jnp.where(lane < HEAD_DIM, q2, zero),
                            jnp.where(lane >= HEAD_DIM, q2, zero)], axis=0)


def _unstack_heads(acc):
    lane = lax.broadcasted_iota(jnp.int32, (SUB, LANES), 1)
    return jnp.where(lane < HEAD_DIM, acc[:SUB], acc[SUB:])


def _windows(blocks, sufw, fillers=()):
    n = len(blocks)
    logits, log_betas, sums, out = [None] * n, [None] * n, [None] * n, [None] * n

    def stage_logits(i):
        qs_fn, k_fn = blocks[i][0], blocks[i][1]
        logits[i] = lax.dot_general(qs_fn(), k_fn(), (((1,), (1,)), ((), ())),
                                    preferred_element_type=F32)

    def stage_suffix_sums(i):
        _, _, _, bias_a, bias_b, _ = blocks[i]
        la, lb = logits[i][:, :HALF], logits[i][:, HALF:]
        if bias_a is not None:
            la = la + bias_a
        if bias_b is not None:
            lb = lb + bias_b
        lg = jnp.concatenate([la, lb], axis=1)
        softplus = jnp.log(1.0 + jnp.exp2(jnp.abs(lg) * (-LOG2_E)))
        log_beta = jnp.minimum(lg, 0.0) - softplus
        l1m = log_beta - lg
        logits[i], log_betas[i] = None, log_beta
        sums[i] = (_dot(l1m.astype(BF16), sufw), jnp.sum(l1m, axis=1, keepdims=True))

    def stage_weights(i):
        (later, total), carry = sums[i], blocks[i][5]
        if carry is None:
            new_carry = total
        else:
            later = later + carry
            new_carry = carry + total
        p = jnp.exp(log_betas[i] + later).astype(BF16)
        sums[i], log_betas[i] = None, None
        out[i] = (_dot(p, blocks[i][2]()), new_carry)

    n_iter = n + 2 * SKEW
    fillers = list(fillers)
    issued = 0
    for i in range(n_iter):
        if i < n:
            stage_logits(i)
        if 0 <= i - SKEW < n:
            stage_suffix_sums(i - SKEW)
        if 0 <= i - 2 * SKEW < n:
            stage_weights(i - 2 * SKEW)
        while issued < len(fillers) and issued * n_iter <= 2 * i * len(fillers):
            fillers[issued]()
            issued += 1
    for f in fillers[issued:]:
        f()
    return out


def _suffix_sum_weights():
    j = jnp.arange(WIN)[:, None]
    c = jnp.arange(WIN)[None, :]
    return (j > c).astype(BF16)


def _proj_attn_kernel(x_ref, g_ref, w_ref, pw_ref, ps_ref, sufw_ref, yp_ref, at_ref,
                      ext_ref, q_s, k_s, v_s, acc_ref, carry_ref, worst_ref):
    s = pl.program_id(1)
    rows = x_ref.shape[0]
    seq0 = s * rows
    pw, aw = POOL_WIDTH, ATTN_WIDTH

    row_in_sub = lax.broadcasted_iota(jnp.int32, (2 * SUB, HALF), 0) & (SUB - 1)
    col_a = lax.broadcasted_iota(jnp.int32, (2 * SUB, HALF), 1)
    col_b = col_a + HALF

    def masked(key_offset_limit):
        return (jnp.where(col_a < key_offset_limit, 0.0, MASKED_LOGIT),
                jnp.where(col_b < key_offset_limit, 0.0, MASKED_LOGIT))

    _, diag_bias = masked((WIN - SUB) + row_in_sub)

    @pl.when(s == 0)
    def _():
        ext_ref[0:HALO, :] = jnp.zeros((HALO, pw), F32)

    def projection_stages(r0):
        sub = slice(r0, r0 + FUSE_SUB)
        seq_rows = pl.ds(pl.multiple_of(seq0 + r0, FUSE_SUB), FUSE_SUB)
        st = {}

        def norm():
            st["h"] = _rms_norm(x_ref[sub, :], g_ref[...]).astype(BF16)

        def pool_values():
            ext_ref[HALO + r0:HALO + r0 + FUSE_SUB, :] = _dot(st["h"], w_ref[:, 0:pw])

        def pool_gate():
            st["z"] = _dot(st["h"], w_ref[:, pw:2 * pw])

        def queries():
            q_s[sub, :] = (_dot(st["h"], w_ref[:, 2 * pw:2 * pw + aw]) * HEAD_DIM ** -0.5).astype(BF16)

        def keys():
            k_s[seq_rows, :] = _dot(st["h"], w_ref[:, 2 * pw + aw:2 * pw + 2 * aw]).astype(BF16)

        def values():
            v_s[seq_rows, :] = _dot(st["h"], w_ref[:, 2 * pw + 2 * aw:2 * pw + 3 * aw]).astype(BF16)

        def pooling():
            head_pos = seq0 + r0 + lax.broadcasted_iota(jnp.int32, (HALO, 1), 0)
            for g, w in enumerate(POOL_WINDOWS):
                cols = slice(g * POOL_GROUP, (g + 1) * POOL_GROUP)
                e = ext_ref[r0:r0 + HALO + FUSE_SUB, cols]
                acc, span = e, 1
                while span < w:
                    acc = acc + pltpu.roll(acc, span, axis=0)
                    span *= 2
                ug, acc = e[HALO:], acc[HALO:]
                cnt = jnp.minimum(head_pos + 1, w).astype(F32)
                pooled = jnp.concatenate([acc[:HALO] / cnt, acc[HALO:] * (1.0 / w)], axis=0) - ug
                mixed = _dot(pooled.astype(BF16), pw_ref[g])
                y = mixed * ps_ref[:, cols] * _silu(st["z"][:, cols])
                yp_ref[sub, cols] = y.astype(BF16)

        return [norm, pool_values, pool_gate, queries, keys, values, pooling]

    def first_window_blocks(r0):
        blocks, slots = [], []
        for row in range(r0, r0 + FUSE_SUB, SUB):
            q0 = seq0 + row
            if row < WIN - SUB:
                start = jnp.maximum(q0 - (WIN - SUB), 0)
                bias_a, bias_b = masked((q0 - start) + row_in_sub)
            else:
                start = q0 - (WIN - SUB)
                bias_a, bias_b = None, diag_bias
            keys = pl.ds(pl.multiple_of(start, SUB), WIN)
            for p in range(N_PAIRS):
                lanes = slice(p * LANES, (p + 1) * LANES)
                blocks.append((functools.partial(lambda r, l: _stack_heads(q_s[r:r + SUB, l]), row, lanes),
                               functools.partial(lambda k, l: k_s[k, l], keys, lanes),
                               functools.partial(lambda k, l: v_s[k, l], keys, lanes),
                               bias_a, bias_b, None))
                slots.append((slice(row, row + SUB), lanes, start if row <= WIN - SUB else None))
        return blocks, slots

    for stage in projection_stages(0):
        stage()
    worst = None
    for r0 in range(0, rows, FUSE_SUB):
        fillers = projection_stages(r0 + FUSE_SUB) if r0 + FUSE_SUB < rows else ()
        blocks, slots = first_window_blocks(r0)
        for (pv, total), (out_rows, lanes, start) in zip(_windows(blocks, sufw_ref[...], fillers), slots):
            at_ref[out_rows, lanes] = _unstack_heads(pv).astype(BF16)
            if start is not None:
                total = jnp.where(start > 0, total, -jnp.inf)
            worst = total if worst is None else jnp.maximum(worst, total)
    worst_ref[0] = jnp.max(worst)
    ext_ref[0:HALO, :] = ext_ref[rows:rows + HALO, :]

    @pl.when(worst_ref[0] > LOG_WEIGHT_FLOOR)
    def _():
        def sweep_rows(blk, _):
            row = pl.multiple_of(blk * SUB, SUB)
            q0 = seq0 + row
            for p in range(N_PAIRS):
                lanes = slice(p * LANES, (p + 1) * LANES)
                qs_fn = lambda: _stack_heads(q_s[pl.ds(row, SUB), lanes])

                def window(start, limit, carry):
                    keys = pl.ds(pl.multiple_of(start, SUB), WIN)
                    block = (qs_fn, lambda: k_s[keys, lanes], lambda: v_s[keys, lanes]) + masked(limit) + (carry,)
                    (pv, total), = _windows([block], sufw_ref[...])
                    return pv, total

                start0 = jnp.maximum(q0 - (WIN - SUB), 0)
                pv, total = window(start0, (q0 - start0) + row_in_sub, None)
                acc_ref[...] = pv
                carry_ref[...] = total

                def cond(c):
                    start, m = c
                    return jnp.logical_and(start > 0, m > LOG_WEIGHT_FLOOR)

                def body(c):
                    start, _ = c
                    nxt = jnp.maximum(start - WIN, 0)
                    pv, total = window(nxt, start - nxt, carry_ref[...])
                    acc_ref[...] = acc_ref[...] + pv
                    carry_ref[...] = total
                    return nxt, jnp.max(total)

                lax.while_loop(cond, body, (start0, jnp.max(total)))
                at_ref[pl.ds(row, SUB), lanes] = _unstack_heads(acc_ref[...]).astype(BF16)
            return 0

        lax.fori_loop(0, rows // SUB, sweep_rows, 0)


def _layer_spec(layer, shape, col_block=0):
    index = (layer,) + (0,) * (len(shape) - 1) + (col_block,)
    return pl.BlockSpec((None,) + tuple(shape), lambda b, s: index, pipeline_mode=pl.Buffered(1))


def _proj_attn(layer, x, norm_g, w_in, pool_w, pool_scale):
    B, S, D = x.shape
    rows = FUSE_ROWS
    row_spec = lambda width: pl.BlockSpec((None, rows, width), lambda b, s: (b, s, 0))
    return pl.pallas_call(
        _proj_attn_kernel,
        grid=(B, S // rows),
        in_specs=[row_spec(D), _layer_spec(layer, (1, D)), _layer_spec(layer, (D, PRE_ATTN_COLS), 0),
                  _layer_spec(layer, (len(POOL_WINDOWS), POOL_GROUP, POOL_GROUP)),
                  _layer_spec(layer, (1, POOL_WIDTH)),
                  pl.BlockSpec((WIN, WIN), lambda b, s: (0, 0), pipeline_mode=pl.Buffered(1))],
        out_specs=[row_spec(POOL_WIDTH), row_spec(ATTN_WIDTH)],
        out_shape=[jax.ShapeDtypeStruct((B, S, POOL_WIDTH), BF16), jax.ShapeDtypeStruct((B, S, ATTN_WIDTH), BF16)],
        scratch_shapes=[pltpu.VMEM((rows + HALO, POOL_WIDTH), F32),
                        pltpu.VMEM((rows, ATTN_WIDTH), BF16),
                        pltpu.VMEM((S, ATTN_WIDTH), BF16),
                        pltpu.VMEM((S, ATTN_WIDTH), BF16),
                        pltpu.VMEM((2 * SUB, LANES), F32),
                        pltpu.VMEM((2 * SUB, 1), F32),
                        pltpu.SMEM((1,), F32)],
        compiler_params=pltpu.CompilerParams(
            dimension_semantics=("arbitrary", "arbitrary"), vmem_limit_bytes=VMEM_LIMIT),
        name="proj_attn",
    )(x, norm_g, w_in, pool_w, pool_scale, _suffix_sum_weights())


def _merge_kernel(x_ref, g_ref, w_ref, b_ref, yp_ref, at_ref, wpu_ref, wau_ref, wo_ref, fg_ref,
                  o_ref, *, final):
    aw, d = ATTN_WIDTH, D_MODEL
    for r0 in range(0, x_ref.shape[0], MERGE_SUB):
        sub = slice(r0, r0 + MERGE_SUB)
        x = x_ref[sub, :]
        h = _rms_norm(x, g_ref[...]).astype(BF16)
        z_attn = _dot(h, w_ref[:, 0:aw])
        y_attn = (at_ref[sub, :].astype(F32) * _silu(z_attn)).astype(BF16)
        gate_pool = jax.nn.sigmoid(_dot(h, w_ref[:, aw:aw + d]) + b_ref[:, 0:d])
        gate_attn = jax.nn.sigmoid(_dot(h, w_ref[:, aw + d:aw + 2 * d]) + b_ref[:, d:2 * d])
        merged = gate_pool * _dot(yp_ref[sub, :], wpu_ref[...]) + gate_attn * _dot(y_attn, wau_ref[...])
        out = x + _dot(merged.astype(BF16), wo_ref[...])
        if final:
            out = _rms_norm(out, fg_ref[...])
        o_ref[sub, :] = out


def _merge(layer, x, norm_g, w_in, b_gate, y_pool, attn, w_pool_up, w_attn_up, w_out, final_g, final):
    B, S, D = x.shape
    rows = MERGE_ROWS
    row_spec = lambda width: pl.BlockSpec((None, rows, width), lambda b, s: (b, s, 0))
    return pl.pallas_call(
        functools.partial(_merge_kernel, final=final),
        grid=(B, S // rows),
        in_specs=[row_spec(D), _layer_spec(layer, (1, D)), _layer_spec(layer, (D, PRE_ATTN_COLS), 1),
                  _layer_spec(layer, (1, 2 * D)), row_spec(POOL_WIDTH), row_spec(ATTN_WIDTH),
                  _layer_spec(layer, (POOL_WIDTH, D)), _layer_spec(layer, (ATTN_WIDTH, D)), _layer_spec(layer, (D, D)),
                  pl.BlockSpec((1, D), lambda b, s: (0, 0), pipeline_mode=pl.Buffered(1))],
        out_specs=row_spec(D),
        out_shape=jax.ShapeDtypeStruct((B, S, D), F32),
        compiler_params=pltpu.CompilerParams(
            dimension_semantics=("arbitrary", "arbitrary"), vmem_limit_bytes=VMEM_LIMIT),
        name="merge_out",
    )(x, norm_g, w_in, b_gate, y_pool, attn, w_pool_up, w_attn_up, w_out, final_g)


def kernel(x, norm_g, w_in, b_gate, pool_w, pool_scale, w_pool_up, w_attn_up, w_out, final_g):
    depth = norm_g.shape[0]
    w_in, pool_w, w_pool_up, w_attn_up, w_out = (a.astype(BF16) for a in (w_in, pool_w, w_pool_up, w_attn_up, w_out))
    norm_g = norm_g.reshape(depth, 1, D_MODEL)
    b_gate = b_gate.reshape(depth, 1, 2 * D_MODEL)
    pool_scale = pool_scale.reshape(depth, 1, POOL_WIDTH)
    final_g = final_g.reshape(1, D_MODEL)
    for l in range(depth):
        y_pool, attn = _proj_attn(l, x, norm_g, w_in, pool_w, pool_scale)
        x = _merge(l, x, norm_g, w_in, b_gate, y_pool, attn, w_pool_up, w_attn_up, w_out, final_g,
                   final=(l == depth - 1))
    return x
```

```python
import functools

import jax
import jax.numpy as jnp
from jax import lax
from jax.experimental import pallas as pl
from jax.experimental.pallas import tpu as pltpu

D_MODEL = 1024
POOL_WIDTH = 512
POOL_WINDOWS = (2, 4, 8, 16)
POOL_GROUP = 128
N_HEADS = 8
HEAD_DIM = 64
ATTN_WIDTH = N_HEADS * HEAD_DIM
PRE_ATTN_COLS = 2 * POOL_WIDTH + 3 * ATTN_WIDTH
RMS_EPS = 1e-6

LANES = 128
N_PAIRS = ATTN_WIDTH // LANES
HALO = 16
FUSE_ROWS = 1024
FUSE_SUB = 256
MERGE_ROWS = 1024
MERGE_SUB = 256
SUB = 64
WIN = 256
HALF = WIN // 2
SKEW = 12
LOG_WEIGHT_FLOOR = -110.0
LOG2_E = 1.4426950408889634
MASKED_LOGIT = -1e30
VMEM_LIMIT = 52 * 1024 * 1024

F32 = jnp.float32
BF16 = jnp.bfloat16


def _rms_norm(x, g):
    ms = jnp.mean(x * x, axis=-1, keepdims=True)
    return x * lax.rsqrt(ms + RMS_EPS) * g


def _silu(z):
    return z * jax.nn.sigmoid(z)


def _dot(a, b):
    return jnp.dot(a, b, preferred_element_type=F32)


def _stack_heads(q2):
    lane = lax.broadcasted_iota(jnp.int32, q2.shape, 1)
    zero = jnp.zeros_like(q2)
    return jnp.concatenate([jnp.where(lane < HEAD_DIM, q2, zero),
                            jnp.where(lane >= HEAD_DIM, q2, zero)], axis=0)


def _unstack_heads(acc):
    lane = lax.broadcasted_iota(jnp.int32, (SUB, LANES), 1)
    return jnp.where(lane < HEAD_DIM, acc[:SUB], acc[SUB:])


def _windows(blocks, sufw, fillers=()):
    n = len(blocks)
    logits, log_betas, sums, out = [None] * n, [None] * n, [None] * n, [None] * n

    def stage_logits(i):
        qs_fn, k_fn = blocks[i][0], blocks[i][1]
        logits[i] = lax.dot_general(qs_fn(), k_fn(), (((1,), (1,)), ((), ())),
                                    preferred_element_type=F32)

    def stage_suffix_sums(i):
        _, _, _, bias_a, bias_b, _ = blocks[i]
        la, lb = logits[i][:, :HALF], logits[i][:, HALF:]
        if bias_a is not None:
            la = la + bias_a
        if bias_b is not None:
            lb = lb + bias_b
        lg = jnp.concatenate([la, lb], axis=1)
        softplus = jnp.log(1.0 + jnp.exp2(jnp.abs(lg) * (-LOG2_E)))
        log_beta = jnp.minimum(lg, 0.0) - softplus
        l1m = log_beta - lg
        logits[i], log_betas[i] = None, log_beta
        sums[i] = (_dot(l1m.astype(BF16), sufw), jnp.sum(l1m, axis=1, keepdims=True))

    def stage_weights(i):
        (later, total), carry = sums[i], blocks[i][5]
        if carry is None:
            new_carry = total
        else:
            later = later + carry
            new_carry = carry + total
        p = jnp.exp(log_betas[i] + later).astype(BF16)
        sums[i], log_betas[i] = None, None
        out[i] = (_dot(p, blocks[i][2]()), new_carry)

    n_iter = n + 2 * SKEW
    fillers = list(fillers)
    issued = 0
    for i in range(n_iter):
        if i < n:
            stage_logits(i)
        if 0 <= i - SKEW < n:
            stage_suffix_sums(i - SKEW)
        if 0 <= i - 2 * SKEW < n:
            stage_weights(i - 2 * SKEW)
        while issued < len(fillers) and issued * n_iter <= 2 * i * len(fillers):
            fillers[issued]()
            issued += 1
    for f in fillers[issued:]:
        f()
    return out


def _suffix_sum_weights():
    j = jnp.arange(WIN)[:, None]
    c = jnp.arange(WIN)[None, :]
    return (j > c).astype(BF16)


def _proj_attn_kernel(x_ref, g_ref, w_ref, pw_ref, ps_ref, sufw_ref, yp_ref, at_ref,
                      ext_ref, q_s, k_s, v_s, acc_ref, carry_ref, worst_ref):
    s = pl.program_id(1)
    rows = x_ref.shape[0]
    seq0 = s * rows
    pw, aw = POOL_WIDTH, ATTN_WIDTH

    row_in_sub = lax.broadcasted_iota(jnp.int32, (2 * SUB, HALF), 0) & (SUB - 1)
    col_a = lax.broadcasted_iota(jnp.int32, (2 * SUB, HALF), 1)
    col_b = col_a + HALF

    def masked(key_offset_limit):
        return (jnp.where(col_a < key_offset_limit, 0.0, MASKED_LOGIT),
                jnp.where(col_b < key_offset_limit, 0.0, MASKED_LOGIT))

    _, diag_bias = masked((WIN - SUB) + row_in_sub)

    @pl.when(s == 0)
    def _():
        ext_ref[0:HALO, :] = jnp.zeros((HALO, pw), F32)

    def projection_stages(r0):
        sub = slice(r0, r0 + FUSE_SUB)
        seq_rows = pl.ds(pl.multiple_of(seq0 + r0, FUSE_SUB), FUSE_SUB)
        st = {}

        def norm():
            st["h"] = _rms_norm(x_ref[sub, :], g_ref[...]).astype(BF16)

        def pool_values():
            ext_ref[HALO + r0:HALO + r0 + FUSE_SUB, :] = _dot(st["h"], w_ref[:, 0:pw])

        def pool_gate():
            st["z"] = _dot(st["h"], w_ref[:, pw:2 * pw])

        def queries():
            q_s[sub, :] = (_dot(st["h"], w_ref[:, 2 * pw:2 * pw + aw]) * HEAD_DIM ** -0.5).astype(BF16)

        def keys():
            k_s[seq_rows, :] = _dot(st["h"], w_ref[:, 2 * pw + aw:2 * pw + 2 * aw]).astype(BF16)

        def values():
            v_s[seq_rows, :] = _dot(st["h"], w_ref[:, 2 * pw + 2 * aw:2 * pw + 3 * aw]).astype(BF16)

        def pooling():
            head_pos = seq0 + r0 + lax.broadcasted_iota(jnp.int32, (HALO, 1), 0)
            for g, w in enumerate(POOL_WINDOWS):
                cols = slice(g * POOL_GROUP, (g + 1) * POOL_GROUP)
                e = ext_ref[r0:r0 + HALO + FUSE_SUB, cols]
                acc, span = e, 1
                while span < w:
                    acc = acc + pltpu.roll(acc, span, axis=0)
                    span *= 2
                ug, acc = e[HALO:], acc[HALO:]
                cnt = jnp.minimum(head_pos + 1, w).astype(F32)
                pooled = jnp.concatenate([acc[:HALO] / cnt, acc[HALO:] * (1.0 / w)], axis=0) - ug
                mixed = _dot(pooled.astype(BF16), pw_ref[g])
                y = mixed * ps_ref[:, cols] * _silu(st["z"][:, cols])
                yp_ref[sub, cols] = y.astype(BF16)

        return [norm, pool_values, pool_gate, queries, keys, values, pooling]

    def first_window_blocks(r0):
        blocks, slots = [], []
        for row in range(r0, r0 + FUSE_SUB, SUB):
            q0 = seq0 + row
            if row < WIN - SUB:
                start = jnp.maximum(q0 - (WIN - SUB), 0)
                bias_a, bias_b = masked((q0 - start) + row_in_sub)
            else:
                start = q0 - (WIN - SUB)
                bias_a, bias_b = None, diag_bias
            keys = pl.ds(pl.multiple_of(start, SUB), WIN)
            for p in range(N_PAIRS):
                lanes = slice(p * LANES, (p + 1) * LANES)
                blocks.append((functools.partial(lambda r, l: _stack_heads(q_s[r:r + SUB, l]), row, lanes),
                               functools.partial(lambda k, l: k_s[k, l], keys, lanes),
                               functools.partial(lambda k, l: v_s[k, l], keys, lanes),
                               bias_a, bias_b, None))
                slots.append((slice(row, row + SUB), lanes, start if row <= WIN - SUB else None))
        return blocks, slots

    for stage in projection_stages(0):
        stage()
    worst = None
    for r0 in range(0, rows, FUSE_SUB):
        fillers = projection_stages(r0 + FUSE_SUB) if r0 + FUSE_SUB < rows else ()
        blocks, slots = first_window_blocks(r0)
        for (pv, total), (out_rows, lanes, start) in zip(_windows(blocks, sufw_ref[...], fillers), slots):
            at_ref[out_rows, lanes] = _unstack_heads(pv).astype(BF16)
            if start is not None:
                total = jnp.where(start > 0, total, -jnp.inf)
            worst = total if worst is None else jnp.maximum(worst, total)
    worst_ref[0] = jnp.max(worst)
    ext_ref[0:HALO, :] = ext_ref[rows:rows + HALO, :]

    @pl.when(worst_ref[0] > LOG_WEIGHT_FLOOR)
    def _():
        def sweep_rows(blk, _):
            row = pl.multiple_of(blk * SUB, SUB)
            q0 = seq0 + row
            for p in range(N_PAIRS):
                lanes = slice(p * LANES, (p + 1) * LANES)
                qs_fn = lambda: _stack_heads(q_s[pl.ds(row, SUB), lanes])

                def window(start, limit, carry):
                    keys = pl.ds(pl.multiple_of(start, SUB), WIN)
                    block = (qs_fn, lambda: k_s[keys, lanes], lambda: v_s[keys, lanes]) + masked(limit) + (carry,)
                    (pv, total), = _windows([block], sufw_ref[...])
                    return pv, total

                start0 = jnp.maximum(q0 - (WIN - SUB), 0)
                pv, total = window(start0, (q0 - start0) + row_in_sub, None)
                acc_ref[...] = pv
                carry_ref[...] = total

                def cond(c):
                    start, m = c
                    return jnp.logical_and(start > 0, m > LOG_WEIGHT_FLOOR)

                def body(c):
                    start, _ = c
                    nxt = jnp.maximum(start - WIN, 0)
                    pv, total = window(nxt, start - nxt, carry_ref[...])
                    acc_ref[...] = acc_ref[...] + pv
                    carry_ref[...] = total
                    return nxt, jnp.max(total)

                lax.while_loop(cond, body, (start0, jnp.max(total)))
                at_ref[pl.ds(row, SUB), lanes] = _unstack_heads(acc_ref[...]).astype(BF16)
            return 0

        lax.fori_loop(0, rows // SUB, sweep_rows, 0)


def _layer_spec(layer, shape, col_block=0):
    index = (layer,) + (0,) * (len(shape) - 1) + (col_block,)
    return pl.BlockSpec((None,) + tuple(shape), lambda b, s: index, pipeline_mode=pl.Buffered(1))


def _proj_attn(layer, x, norm_g, w_in, pool_w, pool_scale):
    B, S, D = x.shape
    rows = FUSE_ROWS
    row_spec = lambda width: pl.BlockSpec((None, rows, width), lambda b, s: (b, s, 0))
    return pl.pallas_call(
        _proj_attn_kernel,
        grid=(B, S // rows),
        in_specs=[row_spec(D), _layer_spec(layer, (1, D)), _layer_spec(layer, (D, PRE_ATTN_COLS), 0),
                  _layer_spec(layer, (len(POOL_WINDOWS), POOL_GROUP, POOL_GROUP)),
                  _layer_spec(layer, (1, POOL_WIDTH)),
                  pl.BlockSpec((WIN, WIN), lambda b, s: (0, 0), pipeline_mode=pl.Buffered(1))],
        out_specs=[row_spec(POOL_WIDTH), row_spec(ATTN_WIDTH)],
        out_shape=[jax.ShapeDtypeStruct((B, S, POOL_WIDTH), BF16), jax.ShapeDtypeStruct((B, S, ATTN_WIDTH), BF16)],
        scratch_shapes=[pltpu.VMEM((rows + HALO, POOL_WIDTH), F32),
                        pltpu.VMEM((rows, ATTN_WIDTH), BF16),
                        pltpu.VMEM((S, ATTN_WIDTH), BF16),
                        pltpu.VMEM((S, ATTN_WIDTH), BF16),
                        pltpu.VMEM((2 * SUB, LANES), F32),
                        pltpu.VMEM((2 * SUB, 1), F32),
                        pltpu.SMEM((1,), F32)],
        compiler_params=pltpu.CompilerParams(
            dimension_semantics=("arbitrary", "arbitrary"), vmem_limit_bytes=VMEM_LIMIT),
        name="proj_attn",
    )(x, norm_g, w_in, pool_w, pool_scale, _suffix_sum_weights())


def _merge_kernel(x_ref, g_ref, w_ref, b_ref, yp_ref, at_ref, wpu_ref, wau_ref, wo_ref, fg_ref,
                  o_ref, *, final):
    aw, d = ATTN_WIDTH, D_MODEL
    for r0 in range(0, x_ref.shape[0], MERGE_SUB):
        sub = slice(r0, r0 + MERGE_SUB)
        x = x_ref[sub, :]
        h = _rms_norm(x, g_ref[...]).astype(BF16)
        z_attn = _dot(h, w_ref[:, 0:aw])
        y_attn = (at_ref[sub, :].astype(F32) * _silu(z_attn)).astype(BF16)
        gate_pool = jax.nn.sigmoid(_dot(h, w_ref[:, aw:aw + d]) + b_ref[:, 0:d])
        gate_attn = jax.nn.sigmoid(_dot(h, w_ref[:, aw + d:aw + 2 * d]) + b_ref[:, d:2 * d])
        merged = gate_pool * _dot(yp_ref[sub, :], wpu_ref[...]) + gate_attn * _dot(y_attn, wau_ref[...])
        out = x + _dot(merged.astype(BF16), wo_ref[...])
        if final:
            out = _rms_norm(out, fg_ref[...])
        o_ref[sub, :] = out


def _merge(layer, x, norm_g, w_in, b_gate, y_pool, attn, w_pool_up, w_attn_up, w_out, final_g, final):
    B, S, D = x.shape
    rows = MERGE_ROWS
    row_spec = lambda width: pl.BlockSpec((None, rows, width), lambda b, s: (b, s, 0))
    return pl.pallas_call(
        functools.partial(_merge_kernel, final=final),
        grid=(B, S // rows),
        in_specs=[row_spec(D), _layer_spec(layer, (1, D)), _layer_spec(layer, (D, PRE_ATTN_COLS), 1),
                  _layer_spec(layer, (1, 2 * D)), row_spec(POOL_WIDTH), row_spec(ATTN_WIDTH),
                  _layer_spec(layer, (POOL_WIDTH, D)), _layer_spec(layer, (ATTN_WIDTH, D)), _layer_spec(layer, (D, D)),
                  pl.BlockSpec((1, D), lambda b, s: (0, 0), pipeline_mode=pl.Buffered(1))],
        out_specs=row_spec(D),
        out_shape=jax.ShapeDtypeStruct((B, S, D), F32),
        compiler_params=pltpu.CompilerParams(
            dimension_semantics=("arbitrary", "arbitrary"), vmem_limit_bytes=VMEM_LIMIT),
        name="merge_out",
    )(x, norm_g, w_in, b_gate, y_pool, attn, w_pool_up, w_attn_up, w_out, final_g)


def kernel(x, norm_g, w_in, b_gate, pool_w, pool_scale, w_pool_up, w_attn_up, w_out, final_g):
    depth = norm_g.shape[0]
    w_in, pool_w, w_pool_up, w_attn_up, w_out = (a.astype(BF16) for a in (w_in, pool_w, w_pool_up, w_attn_up, w_out))
    norm_g = norm_g.reshape(depth, 1, D_MODEL)
    b_gate = b_gate.reshape(depth, 1, 2 * D_MODEL)
    pool_scale = pool_scale.reshape(depth, 1, POOL_WIDTH)
    final_g = final_g.reshape(1, D_MODEL)
    for l in range(depth):
        y_pool, attn = _proj_attn(l, x, norm_g, w_in, pool_w, pool_scale)
        x = _merge(l, x, norm_g, w_in, b_gate, y_pool, attn, w_pool_up, w_attn_up, w_out, final_g,
                   final=(l == depth - 1))
    return x
```

```python
import functools

import jax
import jax.numpy as jnp
from jax import lax
from jax.experimental import pallas as pl
from jax.experimental.pallas import tpu as pltpu

D_MODEL = 1024
POOL_WIDTH = 512
POOL_WINDOWS = (2, 4, 8, 16)
POOL_GROUP = 128
N_HEADS = 8
HEAD_DIM = 64
ATTN_WIDTH = N_HEADS * HEAD_DIM
PRE_ATTN_COLS = 2 * POOL_WIDTH + 3 * ATTN_WIDTH
RMS_EPS = 1e-6

LANES = 128
N_PAIRS = ATTN_WIDTH // LANES
HALO = 16
FUSE_ROWS = 1024
FUSE_SUB = 256
MERGE_ROWS = 1024
MERGE_SUB = 256
SUB = 64
WIN = 256
HALF = WIN // 2
SKEW = 12
LOG_WEIGHT_FLOOR = -110.0
LOG2_E = 1.4426950408889634
MASKED_LOGIT = -1e30
VMEM_LIMIT = 52 * 1024 * 1024

F32 = jnp.float32
BF16 = jnp.bfloat16


def _rms_norm(x, g):
    ms = jnp.mean(x * x, axis=-1, keepdims=True)
    return x * lax.rsqrt(ms + RMS_EPS) * g


def _silu(z):
    return z * jax.nn.sigmoid(z)


def _dot(a, b):
    return jnp.dot(a, b, preferred_element_type=F32)


def _stack_heads(q2):
    lane = lax.broadcasted_iota(jnp.int32, q2.shape, 1)
    zero = jnp.zeros_like(q2)
    return jnp.concatenate([jnp.where(lane < HEAD_DIM, q2, zero),
                            jnp.where(lane >= HEAD_DIM, q2, zero)], axis=0)


def _unstack_heads(acc):
    lane = lax.broadcasted_iota(jnp.int32, (SUB, LANES), 1)
    return jnp.where(lane < HEAD_DIM, acc[:SUB], acc[SUB:])


def _windows(blocks, sufw, fillers=()):
    n = len(blocks)
    logits, log_betas, sums, out = [None] * n, [None] * n, [None] * n, [None] * n

    def stage_logits(i):
        qs_fn, k_fn = blocks[i][0], blocks[i][1]
        logits[i] = lax.dot_general(qs_fn(), k_fn(), (((1,), (1,)), ((), ())),
                                    preferred_element_type=F32)

    def stage_suffix_sums(i):
        _, _, _, bias_a, bias_b, _ = blocks[i]
        la, lb = logits[i][:, :HALF], logits[i][:, HALF:]
        if bias_a is not None:
            la = la + bias_a
        if bias_b is not None:
            lb = lb + bias_b
        lg = jnp.concatenate([la, lb], axis=1)
        softplus = jnp.log(1.0 + jnp.exp2(jnp.abs(lg) * (-LOG2_E)))
        log_beta = jnp.minimum(lg, 0.0) - softplus
        l1m = log_beta - lg
        logits[i], log_betas[i] = None, log_beta
        sums[i] = (_dot(l1m.astype(BF16), sufw), jnp.sum(l1m, axis=1, keepdims=True))

    def stage_weights(i):
        (later, total), carry = sums[i], blocks[i][5]
        if carry is None:
            new_carry = total
        else:
            later = later + carry
            new_carry = carry + total
        p = jnp.exp(log_betas[i] + later).astype(BF16)
        sums[i], log_betas[i] = None, None
        out[i] = (_dot(p, blocks[i][2]()), new_carry)

    n_iter = n + 2 * SKEW
    fillers = list(fillers)
    issued = 0
    for i in range(n_iter):
        if i < n:
            stage_logits(i)
        if 0 <= i - SKEW < n:
            stage_suffix_sums(i - SKEW)
        if 0 <= i - 2 * SKEW < n:
            stage_weights(i - 2 * SKEW)
        while issued < len(fillers) and issued * n_iter <= 2 * i * len(fillers):
            fillers[issued]()
            issued += 1
    for f in fillers[issued:]:
        f()
    return out


def _suffix_sum_weights():
    j = jnp.arange(WIN)[:, None]
    c = jnp.arange(WIN)[None, :]
    return (j > c).astype(BF16)


def _proj_attn_kernel(x_ref, g_ref, w_ref, pw_ref, ps_ref, sufw_ref, yp_ref, at_ref,
                      ext_ref, q_s, k_s, v_s, acc_ref, carry_ref, worst_ref):
    s = pl.program_id(1)
    rows = x_ref.shape[0]
    seq0 = s * rows
    pw, aw = POOL_WIDTH, ATTN_WIDTH

    row_in_sub = lax.broadcasted_iota(jnp.int32, (2 * SUB, HALF), 0) & (SUB - 1)
    col_a = lax.broadcasted_iota(jnp.int32, (2 * SUB, HALF), 1)
    col_b = col_a + HALF

    def masked(key_offset_limit):
        return (jnp.where(col_a < key_offset_limit, 0.0, MASKED_LOGIT),
                jnp.where(col_b < key_offset_limit, 0.0, MASKED_LOGIT))

    _, diag_bias = masked((WIN - SUB) + row_in_sub)

    @pl.when(s == 0)
    def _():
        ext_ref[0:HALO, :] = jnp.zeros((HALO, pw), F32)

    def projection_stages(r0):
        sub = slice(r0, r0 + FUSE_SUB)
        seq_rows = pl.ds(pl.multiple_of(seq0 + r0, FUSE_SUB), FUSE_SUB)
        st = {}

        def norm():
            st["h"] = _rms_norm(x_ref[sub, :], g_ref[...]).astype(BF16)

        def pool_values():
            ext_ref[HALO + r0:HALO + r0 + FUSE_SUB, :] = _dot(st["h"], w_ref[:, 0:pw])

        def pool_gate():
            st["z"] = _dot(st["h"], w_ref[:, pw:2 * pw])

        def queries():
            q_s[sub, :] = (_dot(st["h"], w_ref[:, 2 * pw:2 * pw + aw]) * HEAD_DIM ** -0.5).astype(BF16)

        def keys():
            k_s[seq_rows, :] = _dot(st["h"], w_ref[:, 2 * pw + aw:2 * pw + 2 * aw]).astype(BF16)

        def values():
            v_s[seq_rows, :] = _dot(st["h"], w_ref[:, 2 * pw + 2 * aw:2 * pw + 3 * aw]).astype(BF16)

        def pooling():
            head_pos = seq0 + r0 + lax.broadcasted_iota(jnp.int32, (HALO, 1), 0)
            for g, w in enumerate(POOL_WINDOWS):
                cols = slice(g * POOL_GROUP, (g + 1) * POOL_GROUP)
                e = ext_ref[r0:r0 + HALO + FUSE_SUB, cols]
                acc, span = e, 1
                while span < w:
                    acc = acc + pltpu.roll(acc, span, axis=0)
                    span *= 2
                ug, acc = e[HALO:], acc[HALO:]
                cnt = jnp.minimum(head_pos + 1, w).astype(F32)
                pooled = jnp.concatenate([acc[:HALO] / cnt, acc[HALO:] * (1.0 / w)], axis=0) - ug
                mixed = _dot(pooled.astype(BF16), pw_ref[g])
                y = mixed * ps_ref[:, cols] * _silu(st["z"][:, cols])
                yp_ref[sub, cols] = y.astype(BF16)

        return [norm, pool_values, pool_gate, queries, keys, values, pooling]

    def first_window_blocks(r0):
        blocks, slots = [], []
        for row in range(r0, r0 + FUSE_SUB, SUB):
            q0 = seq0 + row
            if row < WIN - SUB:
                start = jnp.maximum(q0 - (WIN - SUB), 0)
                bias_a, bias_b = masked((q0 - start) + row_in_sub)
            else:
                start = q0 - (WIN - SUB)
                bias_a, bias_b = None, diag_bias
            keys = pl.ds(pl.multiple_of(start, SUB), WIN)
            for p in range(N_PAIRS):
                lanes = slice(p * LANES, (p + 1) * LANES)
                blocks.append((functools.partial(lambda r, l: _stack_heads(q_s[r:r + SUB, l]), row, lanes),
                               functools.partial(lambda k, l: k_s[k, l], keys, lanes),
                               functools.partial(lambda k, l: v_s[k, l], keys, lanes),
                               bias_a, bias_b, None))
                slots.append((slice(row, row + SUB), lanes, start if row <= WIN - SUB else None))
        return blocks, slots

    for stage in projection_stages(0):
        stage()
    worst = None
    for r0 in range(0, rows, FUSE_SUB):
        fillers = projection_stages(r0 + FUSE_SUB) if r0 + FUSE_SUB < rows else ()
        blocks, slots = first_window_blocks(r0)
        for (pv, total), (out_rows, lanes, start) in zip(_windows(blocks, sufw_ref[...], fillers), slots):
            at_ref[out_rows, lanes] = _unstack_heads(pv).astype(BF16)
            if start is not None:
                total = jnp.where(start > 0, total, -jnp.inf)
            worst = total if worst is None else jnp.maximum(worst, total)
    worst_ref[0] = jnp.max(worst)
    ext_ref[0:HALO, :] = ext_ref[rows:rows + HALO, :]

    @pl.when(worst_ref[0] > LOG_WEIGHT_FLOOR)
    def _():
        def sweep_rows(blk, _):
            row = pl.multiple_of(blk * SUB, SUB)
            q0 = seq0 + row
            for p in range(N_PAIRS):
                lanes = slice(p * LANES, (p + 1) * LANES)
                qs_fn = lambda: _stack_heads(q_s[pl.ds(row, SUB), lanes])

                def window(start, limit, carry):
                    keys = pl.ds(pl.multiple_of(start, SUB), WIN)
                    block = (qs_fn, lambda: k_s[keys, lanes], lambda: v_s[keys, lanes]) + masked(limit) + (carry,)
                    (pv, total), = _windows([block], sufw_ref[...])
                    return pv, total

                start0 = jnp.maximum(q0 - (WIN - SUB), 0)
                pv, total = window(start0, (q0 - start0) + row_in_sub, None)
                acc_ref[...] = pv
                carry_ref[...] = total

                def cond(c):
                    start, m = c
                    return jnp.logical_and(start > 0, m > LOG_WEIGHT_FLOOR)

                def body(c):
                    start, _ = c
                    nxt = jnp.maximum(start - WIN, 0)
                    pv, total = window(nxt, start - nxt, carry_ref[...])
                    acc_ref[...] = acc_ref[...] + pv
                    carry_ref[...] = total
                    return nxt, jnp.max(total)

                lax.while_loop(cond, body, (start0, jnp.max(total)))
                at_ref[pl.ds(row, SUB), lanes] = _unstack_heads(acc_ref[...]).astype(BF16)
            return 0

        lax.fori_loop(0, rows // SUB, sweep_rows, 0)


def _layer_spec(layer, shape, col_block=0):
    index = (layer,) + (0,) * (len(shape) - 1) + (col_block,)
    return pl.BlockSpec((None,) + tuple(shape), lambda b, s: index, pipeline_mode=pl.Buffered(1))


def _proj_attn(layer, x, norm_g, w_in, pool_w, pool_scale):
    B, S, D = x.shape
    rows = FUSE_ROWS
    row_spec = lambda width: pl.BlockSpec((None, rows, width), lambda b, s: (b, s, 0))
    return pl.pallas_call(
        _proj_attn_kernel,
        grid=(B, S // rows),
        in_specs=[row_spec(D), _layer_spec(layer, (1, D)), _layer_spec(layer, (D, PRE_ATTN_COLS), 0),
                  _layer_spec(layer, (len(POOL_WINDOWS), POOL_GROUP, POOL_GROUP)),
                  _layer_spec(layer, (1, POOL_WIDTH)),
                  pl.BlockSpec((WIN, WIN), lambda b, s: (0, 0), pipeline_mode=pl.Buffered(1))],
        out_specs=[row_spec(POOL_WIDTH), row_spec(ATTN_WIDTH)],
        out_shape=[jax.ShapeDtypeStruct((B, S, POOL_WIDTH), BF16), jax.ShapeDtypeStruct((B, S, ATTN_WIDTH), BF16)],
        scratch_shapes=[pltpu.VMEM((rows + HALO, POOL_WIDTH), F32),
                        pltpu.VMEM((rows, ATTN_WIDTH), BF16),
                        pltpu.VMEM((S, ATTN_WIDTH), BF16),
                        pltpu.VMEM((S, ATTN_WIDTH), BF16),
                        pltpu.VMEM((2 * SUB, LANES), F32),
                        pltpu.VMEM((2 * SUB, 1), F32),
                        pltpu.SMEM((1,), F32)],
        compiler_params=pltpu.CompilerParams(
            dimension_semantics=("arbitrary", "arbitrary"), vmem_limit_bytes=VMEM_LIMIT),
        name="proj_attn",
    )(x, norm_g, w_in, pool_w, pool_scale, _suffix_sum_weights())


def _merge_kernel(x_ref, g_ref, w_ref, b_ref, yp_ref, at_ref, wpu_ref, wau_ref, wo_ref, fg_ref,
                  o_ref, *, final):
    aw, d = ATTN_WIDTH, D_MODEL
    for r0 in range(0, x_ref.shape[0], MERGE_SUB):
        sub = slice(r0, r0 + MERGE_SUB)
        x = x_ref[sub, :]
        pool_up = _dot(yp_ref[sub, :], wpu_ref[...])
        h = _rms_norm(x, g_ref[...]).astype(BF16)
        z_attn = _dot(h, w_ref[:, 0:aw])
        y_attn = (at_ref[sub, :].astype(F32) * _silu(z_attn)).astype(BF16)
        gate_pool = jax.nn.sigmoid(_dot(h, w_ref[:, aw:aw + d]) + b_ref[:, 0:d])
        gate_attn = jax.nn.sigmoid(_dot(h, w_ref[:, aw + d:aw + 2 * d]) + b_ref[:, d:2 * d])
        merged = gate_pool * pool_up + gate_attn * _dot(y_attn, wau_ref[...])
        out = x + _dot(merged.astype(BF16), wo_ref[...])
        if final:
            out = _rms_norm(out, fg_ref[...])
        o_ref[sub, :] = out


def _merge(layer, x, norm_g, w_in, b_gate, y_pool, attn, w_pool_up, w_attn_up, w_out, final_g, final):
    B, S, D = x.shape
    rows = MERGE_ROWS
    row_spec = lambda width: pl.BlockSpec((None, rows, width), lambda b, s: (b, s, 0))
    return pl.pallas_call(
        functools.partial(_merge_kernel, final=final),
        grid=(B, S // rows),
        in_specs=[row_spec(D), _layer_spec(layer, (1, D)), _layer_spec(layer, (D, PRE_ATTN_COLS), 1),
                  _layer_spec(layer, (1, 2 * D)), row_spec(POOL_WIDTH), row_spec(ATTN_WIDTH),
                  _layer_spec(layer, (POOL_WIDTH, D)), _layer_spec(layer, (ATTN_WIDTH, D)), _layer_spec(layer, (D, D)),
                  pl.BlockSpec((1, D), lambda b, s: (0, 0), pipeline_mode=pl.Buffered(1))],
        out_specs=row_spec(D),
        out_shape=jax.ShapeDtypeStruct((B, S, D), F32),
        compiler_params=pltpu.CompilerParams(
            dimension_semantics=("arbitrary", "arbitrary"), vmem_limit_bytes=VMEM_LIMIT),
        name="merge_out",
    )(x, norm_g, w_in, b_gate, y_pool, attn, w_pool_up, w_attn_up, w_out, final_g)


def kernel(x, norm_g, w_in, b_gate, pool_w, pool_scale, w_pool_up, w_attn_up, w_out, final_g):
    depth = norm_g.shape[0]
    w_in, pool_w, w_pool_up, w_attn_up, w_out = (a.astype(BF16) for a in (w_in, pool_w, w_pool_up, w_attn_up, w_out))
    norm_g = norm_g.reshape(depth, 1, D_MODEL)
    b_gate = b_gate.reshape(depth, 1, 2 * D_MODEL)
    pool_scale = pool_scale.reshape(depth, 1, POOL_WIDTH)
    final_g = final_g.reshape(1, D_MODEL)
    for l in range(depth):
        y_pool, attn = _proj_attn(l, x, norm_g, w_in, pool_w, pool_scale)
        x = _merge(l, x, norm_g, w_in, b_gate, y_pool, attn, w_pool_up, w_attn_up, w_out, final_g,
                   final=(l == depth - 1))
    return x
```
